```python
import jax, jax.numpy as jnp
from jax import lax
import numpy as np

D_MODEL = 1024
BATCH = 16
SEQ = 4096
DEPTH = 4

N_MIXERS = 2
PLE_DIM = 256
ROPE_THETA = 10000.0
NORM_EPS = 1e-6
Q_BLOCK = 128

MLA_HEADS = 8
MLA_Q_RANK = 256
MLA_KV_RANK = 256
MLA_NOPE_DIM = 128
MLA_ROPE_DIM = 64
MLA_V_DIM = 128
MLA_WIDTH = MLA_HEADS * MLA_V_DIM
MLA_SPLITS = [MLA_Q_RANK, MLA_Q_RANK + MLA_KV_RANK, MLA_Q_RANK + MLA_KV_RANK + MLA_ROPE_DIM]
MLA_IN = MLA_Q_RANK + MLA_KV_RANK + MLA_ROPE_DIM + MLA_WIDTH

DSA_HEADS = 8
DSA_KV_HEADS = 2
DSA_HEAD_DIM = 128
DSA_WIDTH = DSA_HEADS * DSA_HEAD_DIM
IDX_HEADS = 8
IDX_DIM = 64
TOPK_MAX = 256
DSA_SIZES = [DSA_WIDTH, DSA_KV_HEADS * DSA_HEAD_DIM, DSA_KV_HEADS * DSA_HEAD_DIM,
             IDX_HEADS * IDX_DIM, IDX_DIM, IDX_HEADS, DSA_WIDTH]
DSA_SPLITS = [int(v) for v in np.cumsum(DSA_SIZES)[:-1]]
DSA_IN = int(sum(DSA_SIZES))

N_MLA_LAYERS = (DEPTH + 1) // 2
N_DSA_LAYERS = DEPTH // 2

kernel_name = "hybrid_mla_dsa_gated_ple"


def rmsnorm(x, g):
    xf = x.astype(jnp.float32)
    y = xf * lax.rsqrt(jnp.mean(xf * xf, axis=-1, keepdims=True) + NORM_EPS)
    return (y * g.astype(jnp.float32)).astype(x.dtype)


def rope(x, pos):
    d = x.shape[-1]
    half = d // 2
    inv = 1.0 / (ROPE_THETA ** (jnp.arange(half, dtype=jnp.float32) / half))
    ang = pos.astype(jnp.float32)[..., None] * inv
    ang = ang.reshape(ang.shape[:2] + (1,) * (x.ndim - 3) + (half,))
    cos, sin = jnp.cos(ang), jnp.sin(ang)
    xf = x.astype(jnp.float32)
    x1, x2 = xf[..., :half], xf[..., half:]
    return jnp.concatenate([x1 * cos - x2 * sin, x2 * cos + x1 * sin], axis=-1).astype(x.dtype)


def sweep_query_blocks(fn, batch, seq):
    out = lax.map(fn, jnp.arange(seq // Q_BLOCK))
    out = jnp.moveaxis(out, 0, 1)
    return out.reshape((batch, seq) + out.shape[3:])


def mla_attention(q_lat, q_pe, c_kv, k_pe):
    B, S = c_kv.shape[:2]
    scale = (MLA_NOPE_DIM + MLA_ROPE_DIM) ** -0.5
    key_pos = jnp.arange(S)

    def block(i):
        start = i * Q_BLOCK
        ql = lax.dynamic_slice_in_dim(q_lat, start, Q_BLOCK, axis=1)
        qp = lax.dynamic_slice_in_dim(q_pe, start, Q_BLOCK, axis=1)
        s = (jnp.einsum('bqhc,bkc->bhqk', ql, c_kv)
             + jnp.einsum('bqhr,bkr->bhqk', qp, k_pe)).astype(jnp.float32) * scale
        qpos = start + jnp.arange(Q_BLOCK)
        causal = key_pos[None, :] <= qpos[:, None]
        s = jnp.where(causal[None, None], s, -jnp.inf)
        pr = jax.nn.softmax(s, axis=-1).astype(c_kv.dtype)
        return jnp.einsum('bhqk,bkc->bqhc', pr, c_kv)

    return sweep_query_blocks(block, B, S)


def mla_branch(h, pos, w_in, q_norm, kv_norm, w_uq, w_uk, w_uv, w_out):
    B, S, _ = h.shape
    c_q, c_kv, k_pe, z = jnp.split(h @ w_in, MLA_SPLITS, axis=-1)
    c_q = rmsnorm(c_q, q_norm)
    c_kv = rmsnorm(c_kv, kv_norm)
    q = (c_q @ w_uq).reshape(B, S, MLA_HEADS, MLA_NOPE_DIM + MLA_ROPE_DIM)
    q_nope, q_pe = q[..., :MLA_NOPE_DIM], q[..., MLA_NOPE_DIM:]
    q_pe = rope(q_pe, pos)
    k_pe = rope(k_pe, pos)
    q_lat = jnp.einsum('bshn,chn->bshc', q_nope, w_uk)
    o_lat = mla_attention(q_lat, q_pe, c_kv, k_pe)
    o = jnp.einsum('bshc,chv->bshv', o_lat, w_uv).reshape(B, S, MLA_WIDTH)
    return (o * jax.nn.silu(z)) @ w_out


def dsa_attention(q, k, v, q_idx, k_idx, w_idx, top_k):
    B, S = k.shape[:2]
    G = DSA_KV_HEADS
    R = DSA_HEADS // DSA_KV_HEADS
    scale = DSA_HEAD_DIM ** -0.5
    key_pos = jnp.arange(S)
    gather = jax.vmap(lambda a, ix: a[ix])

    def block(i):
        start = i * Q_BLOCK
        qpos = start + jnp.arange(Q_BLOCK)
        causal = key_pos[None, :] <= qpos[:, None]
        qi = lax.dynamic_slice_in_dim(q_idx, start, Q_BLOCK, axis=1)
        wi = lax.dynamic_slice_in_dim(w_idx, start, Q_BLOCK, axis=1).astype(jnp.float32)
        logits = jnp.einsum('bqhd,bkd->bqhk', qi, k_idx).astype(jnp.float32) * (IDX_DIM ** -0.5)
        score = jnp.einsum('bqhk,bqh->bqk', jax.nn.relu(logits), wi * (IDX_HEADS ** -0.5))
        score = jnp.where(causal[None], score, -jnp.inf)
        _, idx = lax.top_k(score, top_k)
        valid = idx <= qpos[None, :, None]
        k_sel = gather(k, idx)
        v_sel = gather(v, idx)
        qb = lax.dynamic_slice_in_dim(q, start, Q_BLOCK, axis=1).reshape(B, Q_BLOCK, G, R, DSA_HEAD_DIM)
        s = jnp.einsum('bqgrd,bqkgd->bqgrk', qb, k_sel).astype(jnp.float32) * scale
        s = jnp.where(valid[:, :, None, None, :], s, -jnp.inf)
        pr = jax.nn.softmax(s, axis=-1).astype(v.dtype)
        o = jnp.einsum('bqgrk,bqkgd->bqgrd', pr, v_sel)
        return o.reshape(B, Q_BLOCK, DSA_HEADS, DSA_HEAD_DIM)

    return sweep_query_blocks(block, B, S)


def dsa_branch(h, pos, w_in, w_out):
    B, S, _ = h.shape
    q, k, v, qi, ki, wi, z = jnp.split(h @ w_in, DSA_SPLITS, axis=-1)
    q = rope(q.reshape(B, S, DSA_HEADS, DSA_HEAD_DIM), pos)
    k = rope(k.reshape(B, S, DSA_KV_HEADS, DSA_HEAD_DIM), pos)
    v = v.reshape(B, S, DSA_KV_HEADS, DSA_HEAD_DIM)
    qi = rope(qi.reshape(B, S, IDX_HEADS, IDX_DIM), pos)
    ki = rope(ki, pos)
    top_k = min(TOPK_MAX, S // 4)
    o = dsa_attention(q, k, v, qi, ki, wi, top_k).reshape(B, S, DSA_WIDTH)
    return (o * jax.nn.silu(z)) @ w_out


def setup_inputs(seed: int = 0) -> dict:
    key = jax.random.key(seed)
    ks = jax.random.split(key, 20)

    def w(k, shape, fan_in):
        return jax.random.normal(k, shape, jnp.float32) * (fan_in ** -0.5)

    def gain(k, shape):
        return 1.0 + 0.02 * jax.random.normal(k, shape, jnp.float32)

    x = jax.random.normal(ks[0], (BATCH, SEQ, D_MODEL), jnp.float32)
    p = jax.random.normal(ks[1], (DEPTH, BATCH, SEQ, PLE_DIM), jnp.float32)
    offset = jax.random.randint(ks[2], (BATCH, 1), 0, 1024, dtype=jnp.int32)
    positions = jnp.arange(SEQ, dtype=jnp.int32)[None, :] + offset
    return {
        "x": x,
        "p": p,
        "positions": positions,
        "norm_in": gain(ks[3], (DEPTH, D_MODEL)),
        "mla_w_in": w(ks[4], (N_MLA_LAYERS, D_MODEL, MLA_IN), D_MODEL),
        "mla_q_norm": gain(ks[5], (N_MLA_LAYERS, MLA_Q_RANK)),
        "mla_kv_norm": gain(ks[6], (N_MLA_LAYERS, MLA_KV_RANK)),
        "mla_w_uq": w(ks[7], (N_MLA_LAYERS, MLA_Q_RANK, MLA_HEADS * (MLA_NOPE_DIM + MLA_ROPE_DIM)), MLA_Q_RANK),
        "mla_w_uk": w(ks[8], (N_MLA_LAYERS, MLA_KV_RANK, MLA_HEADS, MLA_NOPE_DIM), MLA_KV_RANK),
        "mla_w_uv": w(ks[9], (N_MLA_LAYERS, MLA_KV_RANK, MLA_HEADS, MLA_V_DIM), MLA_KV_RANK),
        "mla_w_out": w(ks[10], (N_MLA_LAYERS, MLA_WIDTH, D_MODEL), MLA_WIDTH),
        "dsa_w_in": w(ks[11], (N_DSA_LAYERS, D_MODEL, DSA_IN), D_MODEL),
        "dsa_w_out": w(ks[12], (N_DSA_LAYERS, DSA_WIDTH, D_MODEL), DSA_WIDTH),
        "ple_w_proj": w(ks[13], (DEPTH, PLE_DIM, D_MODEL), PLE_DIM),
        "ple_norm": gain(ks[14], (DEPTH, D_MODEL)),
        "ple_w_gate": w(ks[15], (DEPTH, D_MODEL, D_MODEL), D_MODEL),
        "final_norm": gain(ks[16], (D_MODEL,)),
    }


def reference(x, p, positions, norm_in, mla_w_in, mla_q_norm, mla_kv_norm, mla_w_uq,
              mla_w_uk, mla_w_uv, mla_w_out, dsa_w_in, dsa_w_out, ple_w_proj,
              ple_norm, ple_w_gate, final_norm):
    for i in range(DEPTH):
        h = rmsnorm(x, norm_in[i])
        j = i // N_MIXERS
        if i % N_MIXERS == 0:
            y = mla_branch(h, positions, mla_w_in[j], mla_q_norm[j], mla_kv_norm[j],
                           mla_w_uq[j], mla_w_uk[j], mla_w_uv[j], mla_w_out[j])
        else:
            y = dsa_branch(h, positions, dsa_w_in[j], dsa_w_out[j])
        x = x + y
        gate = jax.nn.sigmoid(rmsnorm(x, ple_norm[i]) @ ple_w_gate[i])
        x = x + (p[i] @ ple_w_proj[i]) * gate
    return rmsnorm(x, final_norm)
```

```python
import functools

import jax
import jax.numpy as jnp
import numpy as np
from jax import lax
from jax.experimental import pallas as pl
from jax.experimental.pallas import tpu as pltpu

D_MODEL = 1024
DEPTH = 4
PLE_DIM = 256
ROPE_THETA = 10000.0
NORM_EPS = 1e-6

MLA_HEADS = 8
MLA_Q_RANK = 256
MLA_KV_RANK = 256
MLA_NOPE_DIM = 128
MLA_ROPE_DIM = 64
MLA_V_DIM = 128
MLA_WIDTH = MLA_HEADS * MLA_V_DIM
MLA_QK_PAD = MLA_KV_RANK + 128

DSA_HEADS = 8
DSA_KV_HEADS = 2
DSA_GROUP = DSA_HEADS // DSA_KV_HEADS
DSA_HEAD_DIM = 128
DSA_WIDTH = DSA_HEADS * DSA_HEAD_DIM
IDX_HEADS = 8
IDX_DIM = 64
TOPK_MAX = 256

LANES = 128
NEG = -1e30
VMEM_LIMIT = 56 * 1024 * 1024

ROW_TILE = 256
Q_TILE = 128
K_CHUNK = 512

NT_DIMS = (((1,), (1,)), ((), ()))
BF16 = jnp.bfloat16
F32 = jnp.float32


def _rms(x, g):
    return x * lax.rsqrt(jnp.mean(x * x, axis=-1, keepdims=True) + NORM_EPS) * g


def _params(n_parallel=2):
    return pltpu.CompilerParams(
        dimension_semantics=("arbitrary",) * n_parallel,
        vmem_limit_bytes=VMEM_LIMIT)


def _rope_table_kernel(pos_row_ref, pos_col_ref, inv_col_ref, inv_row_ref, sgn_row_ref,
                       cos_t_ref, sin_t_ref, cos_r_ref, sin_r_ref):
    ang_t = inv_col_ref[...] * pos_row_ref[0]
    cos_t_ref[0] = jnp.cos(ang_t)
    sin_t_ref[0] = jnp.sin(ang_t)
    ang_r = pos_col_ref[0] * inv_row_ref[...]
    cos_r_ref[0] = jnp.cos(ang_r)
    sin_r_ref[0] = jnp.sin(ang_r) * sgn_row_ref[...]


def _rope_tables(positions):
    B, S = positions.shape
    T = 512
    posf = positions.astype(F32)
    inv64 = 1.0 / (ROPE_THETA ** (jnp.arange(64, dtype=F32) / 64))
    inv32 = 1.0 / (ROPE_THETA ** (jnp.arange(32, dtype=F32) / 32))
    inv_col = jnp.concatenate([inv64, inv32]).reshape(96, 1)
    inv_row = jnp.tile(inv32, 4).reshape(1, LANES)
    sgn_row = jnp.tile(jnp.concatenate([-jnp.ones(32, F32), jnp.ones(32, F32)]), 2).reshape(1, LANES)
    full = lambda shape: pl.BlockSpec(shape, lambda b, t: (0,) * len(shape))
    return pl.pallas_call(
        _rope_table_kernel,
        grid=(B, S // T),
        in_specs=[
            pl.BlockSpec((1, 1, T), lambda b, t: (b, 0, t)),
            pl.BlockSpec((1, T, 1), lambda b, t: (b, t, 0)),
            full((96, 1)), full((1, LANES)), full((1, LANES)),
        ],
        out_specs=[
            pl.BlockSpec((1, 96, T), lambda b, t: (b, 0, t)),
            pl.BlockSpec((1, 96, T), lambda b, t: (b, 0, t)),
            pl.BlockSpec((1, T, LANES), lambda b, t: (b, t, 0)),
            pl.BlockSpec((1, T, LANES), lambda b, t: (b, t, 0)),
        ],
        out_shape=[
            jax.ShapeDtypeStruct((B, 96, S), F32),
            jax.ShapeDtypeStruct((B, 96, S), F32),
            jax.ShapeDtypeStruct((B, S, LANES), F32),
            jax.ShapeDtypeStruct((B, S, LANES), F32),
        ],
        compiler_params=_params(),
        name="rope_tables",
    )(posf.reshape(B, 1, S), posf.reshape(B, S, 1), inv_col, inv_row, sgn_row)


def _rope_rows(x, cos_r, sin_r):
    lane = lax.broadcasted_iota(jnp.int32, x.shape, 1)
    first = (lane & 63) < 32
    partner = jnp.where(first, pltpu.roll(x, 96, 1), pltpu.roll(x, 32, 1))
    return x * cos_r + partner * sin_r


def _rope_cols(x, cos_t, sin_t):
    half = x.shape[0] // 2
    x1, x2 = x[:half], x[half:]
    return x1 * cos_t - x2 * sin_t, x2 * cos_t + x1 * sin_t


def _mla_proj_kernel(x_ref, cos_r_ref, sin_r_ref, g_ref, w_in_ref, qn_ref, kvn_ref,
                     w_uq_ref, w_uk_ref, q_ref, kv_ref, z_ref):
    hb = _rms(x_ref[0], g_ref[...]).astype(BF16)
    pr = jnp.dot(hb, w_in_ref[...], preferred_element_type=F32)
    cos_r, sin_r = cos_r_ref[0], sin_r_ref[0]
    c_q = _rms(pr[:, :256], qn_ref[...]).astype(BF16)
    c_kv = _rms(pr[:, 256:512], kvn_ref[...])
    z_ref[0] = pr[:, 512:1536].astype(BF16)
    kv_ref[0, :, :256] = c_kv.astype(BF16)
    kv_ref[0, :, 256:] = _rope_rows(pr[:, 1536:1664], cos_r, sin_r).astype(BF16)
    q = jnp.dot(c_q, w_uq_ref[...], preferred_element_type=F32)
    for h in range(MLA_HEADS):
        q_nope = q[:, h * 128:(h + 1) * 128].astype(BF16)
        q_lat = jnp.dot(q_nope, w_uk_ref[h], preferred_element_type=F32)
        q_ref[0, h, :, :256] = q_lat.astype(BF16)
        q_pe = q[:, 1024 + h * 128:1024 + (h + 1) * 128]
        q_ref[0, h, :, 256:] = _rope_rows(q_pe, cos_r, sin_r).astype(BF16)


def _mla_proj(x, cos_r, sin_r, g, w_in, qn, kvn, w_uq, w_uk):
    B, S, _ = x.shape
    T = ROW_TILE
    w_in_r = jnp.concatenate(
        [w_in[:, :512], w_in[:, 576:], w_in[:, 512:576], jnp.zeros((D_MODEL, 64), w_in.dtype)],
        axis=1).astype(BF16)
    w_uq3 = w_uq.reshape(MLA_Q_RANK, MLA_HEADS, MLA_NOPE_DIM + MLA_ROPE_DIM)
    w_pe = jnp.concatenate([w_uq3[:, :, MLA_NOPE_DIM:], jnp.zeros((MLA_Q_RANK, MLA_HEADS, 64), w_uq.dtype)], axis=2)
    w_uq_r = jnp.concatenate(
        [w_uq3[:, :, :MLA_NOPE_DIM].reshape(MLA_Q_RANK, -1), w_pe.reshape(MLA_Q_RANK, -1)],
        axis=1).astype(BF16)
    w_uk_r = jnp.transpose(w_uk, (1, 2, 0)).astype(BF16)
    full = lambda shape: pl.BlockSpec(shape, lambda b, t: (0,) * len(shape))
    return pl.pallas_call(
        _mla_proj_kernel,
        grid=(B, S // T),
        in_specs=[
            pl.BlockSpec((1, T, D_MODEL), lambda b, t: (b, t, 0)),
            pl.BlockSpec((1, T, LANES), lambda b, t: (b, t, 0)),
            pl.BlockSpec((1, T, LANES), lambda b, t: (b, t, 0)),
            full((1, D_MODEL)), full(w_in_r.shape), full((1, 256)), full((1, 256)),
            full(w_uq_r.shape), full(w_uk_r.shape),
        ],
        out_specs=[
            pl.BlockSpec((1, MLA_HEADS, T, MLA_QK_PAD), lambda b, t: (b, 0, t, 0)),
            pl.BlockSpec((1, T, MLA_QK_PAD), lambda b, t: (b, t, 0)),
            pl.BlockSpec((1, T, MLA_WIDTH), lambda b, t: (b, t, 0)),
        ],
        out_shape=[
            jax.ShapeDtypeStruct((B, MLA_HEADS, S, MLA_QK_PAD), BF16),
            jax.ShapeDtypeStruct((B, S, MLA_QK_PAD), BF16),
            jax.ShapeDtypeStruct((B, S, MLA_WIDTH), BF16),
        ],
        compiler_params=_params(),
        name="mla_proj",
    )(x, cos_r, sin_r, g.reshape(1, -1), w_in_r, qn.reshape(1, -1), kvn.reshape(1, -1), w_uq_r, w_uk_r)


def _mla_attn_kernel(q_ref, kv_ref, w_uv_ref, o_ref, m_ref, l_ref, acc_ref):
    qb = pl.program_id(1)
    rows = MLA_HEADS * Q_TILE
    scale = (MLA_NOPE_DIM + MLA_ROPE_DIM) ** -0.5
    q = q_ref[0].reshape(rows, MLA_QK_PAD)
    m_ref[...] = jnp.full(m_ref.shape, NEG, F32)
    l_ref[...] = jnp.zeros(l_ref.shape, F32)
    acc_ref[...] = jnp.zeros(acc_ref.shape, F32)

    def step(c, masked):
        start = pl.multiple_of(c * K_CHUNK, K_CHUNK)
        kv = kv_ref[0, pl.ds(start, K_CHUNK), :]
        s = lax.dot_general(q, kv, NT_DIMS, preferred_element_type=F32) * scale
        if masked:
            kpos = start + lax.broadcasted_iota(jnp.int32, (Q_TILE, K_CHUNK), 1)
            qpos = qb * Q_TILE + lax.broadcasted_iota(jnp.int32, (Q_TILE, K_CHUNK), 0)
            keep = (kpos <= qpos)[None]
            s = jnp.where(keep, s.reshape(MLA_HEADS, Q_TILE, K_CHUNK), NEG).reshape(rows, K_CHUNK)
        m_old = m_ref[...]
        m_new = jnp.maximum(m_old, jnp.max(s, axis=-1, keepdims=True))
        p = jnp.exp(s - m_new)
        alpha = jnp.exp(m_old - m_new)
        l_ref[...] = alpha * l_ref[...] + jnp.sum(p, axis=-1, keepdims=True)
        pv = jnp.dot(p.astype(BF16), kv[:, :MLA_KV_RANK], preferred_element_type=F32)
        acc_ref[...] = alpha * acc_ref[...] + pv
        m_ref[...] = m_new

    n_full = qb // (K_CHUNK // Q_TILE)

    def body(c, carry):
        step(c, False)
        return carry

    lax.fori_loop(0, n_full, body, 0)
    step(n_full, True)

    o_lat = acc_ref[...] / l_ref[...]
    for h in range(MLA_HEADS):
        o_h = jnp.dot(o_lat[h * Q_TILE:(h + 1) * Q_TILE].astype(BF16), w_uv_ref[h],
                      preferred_element_type=F32)
        o_ref[0, :, h * MLA_V_DIM:(h + 1) * MLA_V_DIM] = o_h.astype(BF16)


def _mla_attn(q_cat, kv_cat, w_uv):
    B, H, S, _ = q_cat.shape
    w_uv_r = jnp.transpose(w_uv, (1, 0, 2)).astype(BF16)
    rows = H * Q_TILE
    return pl.pallas_call(
        _mla_attn_kernel,
        grid=(B, S // Q_TILE),
        in_specs=[
            pl.BlockSpec((1, H, Q_TILE, MLA_QK_PAD), lambda b, i: (b, 0, i, 0)),
            pl.BlockSpec((1, S, MLA_QK_PAD), lambda b, i: (b, 0, 0)),
            pl.BlockSpec(w_uv_r.shape, lambda b, i: (0, 0, 0)),
        ],
        out_specs=pl.BlockSpec((1, Q_TILE, MLA_WIDTH), lambda b, i: (b, i, 0)),
        out_shape=jax.ShapeDtypeStruct((B, S, MLA_WIDTH), BF16),
        scratch_shapes=[
            pltpu.VMEM((rows, 1), F32),
            pltpu.VMEM((rows, 1), F32),
            pltpu.VMEM((rows, MLA_KV_RANK), F32),
        ],
        compiler_params=_params(),
        name="mla_attn",
    )(q_cat, kv_cat, w_uv_r)


DSA_ROW_Q = 0
DSA_ROW_QI = DSA_ROW_Q + DSA_WIDTH
DSA_ROW_K = DSA_ROW_QI + IDX_HEADS * IDX_DIM
DSA_ROW_KI = DSA_ROW_K + DSA_KV_HEADS * DSA_HEAD_DIM
DSA_ROW_V = DSA_ROW_KI + 128
DSA_ROW_WI = DSA_ROW_V + DSA_KV_HEADS * DSA_HEAD_DIM
DSA_ROWS = DSA_ROW_WI + 16


def _dsa_proj_kernel(x_ref, cos_t_ref, sin_t_ref, g_ref, w_f_ref, w_z_ref,
                     qt_ref, qit_ref, k_ref, ki_ref, vt_ref, wit_ref, z_ref):
    hb = _rms(x_ref[0], g_ref[...]).astype(BF16)
    z_ref[0] = jnp.dot(hb, w_z_ref[...], preferred_element_type=F32).astype(BF16)
    pt = lax.dot_general(w_f_ref[...], hb, NT_DIMS, preferred_element_type=F32)
    c64, s64 = cos_t_ref[0, :64], sin_t_ref[0, :64]
    c32, s32 = cos_t_ref[0, 64:96], sin_t_ref[0, 64:96]
    for h in range(DSA_HEADS):
        r = DSA_ROW_Q + h * 128
        o1, o2 = _rope_cols(pt[r:r + 128], c64, s64)
        qt_ref[0, r:r + 64] = o1.astype(BF16)
        qt_ref[0, r + 64:r + 128] = o2.astype(BF16)
    for h in range(IDX_HEADS):
        r = DSA_ROW_QI + h * 64
        o1, o2 = _rope_cols(pt[r:r + 64], c32, s32)
        qit_ref[0, h * 64:h * 64 + 32] = o1.astype(BF16)
        qit_ref[0, h * 64 + 32:h * 64 + 64] = o2.astype(BF16)
    for g in range(DSA_KV_HEADS):
        r = DSA_ROW_K + g * 128
        o1, o2 = _rope_cols(pt[r:r + 128], c64, s64)
        k_ref[0, g] = jnp.concatenate([o1, o2], axis=0).T.astype(BF16)
    o1, o2 = _rope_cols(pt[DSA_ROW_KI:DSA_ROW_KI + 64], c32, s32)
    ki_pad = jnp.concatenate([o1, o2, pt[DSA_ROW_KI + 64:DSA_ROW_KI + 128]], axis=0)
    ki_ref[0] = ki_pad.T[:, :IDX_DIM].astype(BF16)
    vt_ref[0] = pt[DSA_ROW_V:DSA_ROW_V + 256].astype(BF16)
    wit_ref[0] = pt[DSA_ROW_WI:DSA_ROW_WI + 8] * (IDX_HEADS ** -0.5)


def _dsa_proj(x, cos_t, sin_t, g, w_in):
    B, S, _ = x.shape
    T = ROW_TILE
    c = 0
    q_w = w_in[:, c:c + 1024]; c += 1024
    k_w = w_in[:, c:c + 256]; c += 256
    v_w = w_in[:, c:c + 256]; c += 256
    qi_w = w_in[:, c:c + 512]; c += 512
    ki_w = w_in[:, c:c + 64]; c += 64
    wi_w = w_in[:, c:c + 8]; c += 8
    z_w = w_in[:, c:c + 1024]
    zeros = lambda n: jnp.zeros((D_MODEL, n), w_in.dtype)
    w_f = jnp.concatenate([q_w, qi_w, k_w, ki_w, zeros(64), v_w, wi_w, zeros(8)], axis=1).T.astype(BF16)
    assert w_f.shape == (DSA_ROWS, D_MODEL)
    full = lambda shape: pl.BlockSpec(shape, lambda b, t: (0,) * len(shape))
    return pl.pallas_call(
        _dsa_proj_kernel,
        grid=(B, S // T),
        in_specs=[
            pl.BlockSpec((1, T, D_MODEL), lambda b, t: (b, t, 0)),
            pl.BlockSpec((1, 96, T), lambda b, t: (b, 0, t)),
            pl.BlockSpec((1, 96, T), lambda b, t: (b, 0, t)),
            full((1, D_MODEL)), full(w_f.shape), full((D_MODEL, DSA_WIDTH)),
        ],
        out_specs=[
            pl.BlockSpec((1, DSA_WIDTH, T), lambda b, t: (b, 0, t)),
            pl.BlockSpec((1, IDX_HEADS * IDX_DIM, T), lambda b, t: (b, 0, t)),
            pl.BlockSpec((1, DSA_KV_HEADS, T, DSA_HEAD_DIM), lambda b, t: (b, 0, t, 0)),
            pl.BlockSpec((1, T, IDX_DIM), lambda b, t: (b, t, 0)),
            pl.BlockSpec((1, DSA_KV_HEADS * DSA_HEAD_DIM, T), lambda b, t: (b, 0, t)),
            pl.BlockSpec((1, IDX_HEADS, T), lambda b, t: (b, 0, t)),
            pl.BlockSpec((1, T, DSA_WIDTH), lambda b, t: (b, t, 0)),
        ],
        out_shape=[
            jax.ShapeDtypeStruct((B, DSA_WIDTH, S), BF16),
            jax.ShapeDtypeStruct((B, IDX_HEADS * IDX_DIM, S), BF16),
            jax.ShapeDtypeStruct((B, DSA_KV_HEADS, S, DSA_HEAD_DIM), BF16),
            jax.ShapeDtypeStruct((B, S, IDX_DIM), BF16),
            jax.ShapeDtypeStruct((B, DSA_KV_HEADS * DSA_HEAD_DIM, S), BF16),
            jax.ShapeDtypeStruct((B, IDX_HEADS, S), F32),
            jax.ShapeDtypeStruct((B, S, DSA_WIDTH), BF16),
        ],
        compiler_params=_params(),
        name="dsa_proj",
    )(x, cos_t, sin_t, g.reshape(1, -1), w_f, z_w.astype(BF16))


def _key_to_float(key):
    bits = key ^ ((key >> 31) & jnp.int32(0x7FFFFFFF))
    return lax.bitcast_convert_type(bits, F32)


def _dsa_attn_kernel(qt_ref, qit_ref, wit_ref, k_ref, vt_ref, ki_ref, o_ref,
                     sc_ref, m_ref, l_ref, acc_ref, *, top_k):
    qb = pl.program_id(1)
    n_chunks = qb // (K_CHUNK // Q_TILE) + 1
    qpos = qb * Q_TILE + lax.broadcasted_iota(jnp.int32, (K_CHUNK, Q_TILE), 1)

    def chunk_start(c):
        return pl.multiple_of(c * K_CHUNK, K_CHUNK)

    qi_all = jnp.concatenate(
        [qit_ref[0, h * IDX_DIM:(h + 1) * IDX_DIM, :] for h in range(IDX_HEADS)], axis=1)
    w_rows = [wit_ref[0, h:h + 1, :] for h in range(IDX_HEADS)]

    def score_body(c, carry):
        start = chunk_start(c)
        ki = ki_ref[0, pl.ds(start, K_CHUNK), :]
        lg = jnp.dot(ki, qi_all, preferred_element_type=F32) * (IDX_DIM ** -0.5)
        sc = jnp.zeros((K_CHUNK, Q_TILE), F32)
        for h in range(IDX_HEADS):
            sc = sc + jnp.maximum(lg[:, h * Q_TILE:(h + 1) * Q_TILE], 0.0) * w_rows[h]
        kpos = start + lax.broadcasted_iota(jnp.int32, (K_CHUNK, Q_TILE), 0)
        sc_ref[pl.ds(start, K_CHUNK), :] = jnp.where(kpos <= qpos, sc, -jnp.inf)
        return carry

    lax.fori_loop(0, n_chunks, score_body, 0)

    def count(pred):
        def body(c, cnt):
            blk = sc_ref[pl.ds(chunk_start(c), K_CHUNK), :]
            hit = jnp.where(pred(blk), 1.0, 0.0)
            return cnt + jnp.sum(hit.reshape(K_CHUNK // 8, 8, Q_TILE), axis=0)
        cnt = lax.fori_loop(0, n_chunks, body, jnp.zeros((8, Q_TILE), F32))
        return jnp.sum(cnt, axis=0, keepdims=True)

    def search(_):
        def bit_body(i, t):
            cand = t + lax.shift_left(jnp.int32(1), 31 - i)
            cf = _key_to_float(cand)
            n = count(lambda blk: blk >= cf)
            return jnp.where(n >= top_k, cand, t)
        t = lax.fori_loop(0, 32, bit_body, jnp.full((1, Q_TILE), jnp.iinfo(jnp.int32).min, jnp.int32))
        return _key_to_float(t)

    def take_all(_):
        return jnp.full((1, Q_TILE), jnp.finfo(F32).min, F32)

    thr = lax.cond((qb + 1) * Q_TILE > top_k, search, take_all, 0)
    n_ge = count(lambda blk: blk >= thr)
    has_tie = jnp.max(jnp.where(n_ge > top_k, 1.0, 0.0)) > 0.0

    @pl.when(jnp.logical_not(has_tie))
    def _():
        def body(c, carry):
            sl = pl.ds(chunk_start(c), K_CHUNK)
            sc_ref[sl, :] = jnp.where(sc_ref[sl, :] >= thr, 0.0, NEG)
            return carry
        lax.fori_loop(0, n_chunks, body, 0)

    @pl.when(has_tie)
    def _():
        n_gt = count(lambda blk: blk > thr)
        need = top_k - n_gt
        r_i = lax.broadcasted_iota(jnp.int32, (K_CHUNK, K_CHUNK), 0)
        c_i = lax.broadcasted_iota(jnp.int32, (K_CHUNK, K_CHUNK), 1)
        before = jnp.where(c_i < r_i, 1.0, 0.0).astype(BF16)

        def body(c, seen):
            sl = pl.ds(chunk_start(c), K_CHUNK)
            blk = sc_ref[sl, :]
            eq = jnp.where(blk == thr, 1.0, 0.0)
            rank = seen + jnp.dot(before, eq.astype(BF16), preferred_element_type=F32)
            keep = jnp.where(blk > thr, 1.0, jnp.where(rank < need, eq, 0.0))
            sc_ref[sl, :] = jnp.where(keep > 0.0, 0.0, NEG)
            return seen + jnp.sum(eq, axis=0, keepdims=True)
        lax.fori_loop(0, n_chunks, body, jnp.zeros((1, Q_TILE), F32))

    scale = DSA_HEAD_DIM ** -0.5
    cols = DSA_GROUP * Q_TILE
    for g in range(DSA_KV_HEADS):
        q_g = jnp.concatenate(
            [qt_ref[0, (g * DSA_GROUP + j) * DSA_HEAD_DIM:(g * DSA_GROUP + j + 1) * DSA_HEAD_DIM, :]
             for j in range(DSA_GROUP)], axis=1)
        m_ref[...] = jnp.full(m_ref.shape, NEG, F32)
        l_ref[...] = jnp.zeros(l_ref.shape, F32)
        acc_ref[...] = jnp.zeros(acc_ref.shape, F32)

        def attn_body(c, carry):
            start = chunk_start(c)
            k = k_ref[0, g, pl.ds(start, K_CHUNK), :]
            bias = sc_ref[pl.ds(start, K_CHUNK), :]
            s = jnp.dot(k, q_g, preferred_element_type=F32) * scale
            s = s + jnp.concatenate([bias] * DSA_GROUP, axis=1)
            m_old = m_ref[...]
            m_new = jnp.maximum(m_old, jnp.max(s, axis=0, keepdims=True))
            p = jnp.exp(s - m_new)
            alpha = jnp.exp(m_old - m_new)
            l_ref[...] = alpha * l_ref[...] + jnp.sum(p, axis=0, keepdims=True)
            vt = vt_ref[0, g * DSA_HEAD_DIM:(g + 1) * DSA_HEAD_DIM, pl.ds(start, K_CHUNK)]
            acc_ref[...] = alpha * acc_ref[...] + jnp.dot(vt, p.astype(BF16), preferred_element_type=F32)
            m_ref[...] = m_new
            return carry

        lax.fori_loop(0, n_chunks, attn_body, 0)
        o_t = acc_ref[...] / l_ref[...]
        for j in range(DSA_GROUP):
            h = g * DSA_GROUP + j
            o_ref[0, :, h * DSA_HEAD_DIM:(h + 1) * DSA_HEAD_DIM] = (
                o_t[:, j * Q_TILE:(j + 1) * Q_TILE].T.astype(BF16))


def _dsa_attn(qt, qit, wit, k, vt, ki):
    B, _, S = qt.shape
    top_k = min(TOPK_MAX, S // 4)
    cols = DSA_GROUP * Q_TILE
    return pl.pallas_call(
        functools.partial(_dsa_attn_kernel, top_k=top_k),
        grid=(B, S // Q_TILE),
        in_specs=[
            pl.BlockSpec((1, DSA_WIDTH, Q_TILE), lambda b, i: (b, 0, i)),
            pl.BlockSpec((1, IDX_HEADS * IDX_DIM, Q_TILE), lambda b, i: (b, 0, i)),
            pl.BlockSpec((1, IDX_HEADS, Q_TILE), lambda b, i: (b, 0, i)),
            pl.BlockSpec((1, DSA_KV_HEADS, S, DSA_HEAD_DIM), lambda b, i: (b, 0, 0, 0)),
            pl.BlockSpec((1, DSA_KV_HEADS * DSA_HEAD_DIM, S), lambda b, i: (b, 0, 0)),
            pl.BlockSpec((1, S, IDX_DIM), lambda b, i: (b, 0, 0)),
        ],
        out_specs=pl.BlockSpec((1, Q_TILE, DSA_WIDTH), lambda b, i: (b, i, 0)),
        out_shape=jax.ShapeDtypeStruct((B, S, DSA_WIDTH), BF16),
        scratch_shapes=[
            pltpu.VMEM((S, Q_TILE), F32),
            pltpu.VMEM((1, cols), F32),
            pltpu.VMEM((1, cols), F32),
            pltpu.VMEM((DSA_HEAD_DIM, cols), F32),
        ],
        compiler_params=_params(),
        name="dsa_attn",
    )(qt, qit, wit, k, vt, ki)


def _post_kernel(o_ref, z_ref, x_ref, p_ref, w_out_ref, pn_ref, w_gate_ref, w_proj_ref, fn_ref,
                 y_ref, *, final):
    z = z_ref[0].astype(F32)
    gated = o_ref[0].astype(F32) * (z / (1.0 + jnp.exp(-z)))
    x1 = x_ref[0] + jnp.dot(gated.astype(BF16), w_out_ref[...], preferred_element_type=F32)
    pre = jnp.dot(_rms(x1, pn_ref[...]).astype(BF16), w_gate_ref[...], preferred_element_type=F32)
    gate = 1.0 / (1.0 + jnp.exp(-pre))
    emb = jnp.dot(p_ref[0, 0].astype(BF16), w_proj_ref[...], preferred_element_type=F32)
    x2 = x1 + emb * gate
    y_ref[0] = _rms(x2, fn_ref[...]) if final else x2


def _post(o, z, x, p, layer, w_out, pn, w_gate, w_proj, fn, final):
    B, S, _ = x.shape
    T = ROW_TILE
    full = lambda shape: pl.BlockSpec(shape, lambda b, t: (0,) * len(shape))
    tok = lambda d: pl.BlockSpec((1, T, d), lambda b, t: (b, t, 0))
    return pl.pallas_call(
        functools.partial(_post_kernel, final=final),
        grid=(B, S // T),
        in_specs=[
            tok(D_MODEL), tok(D_MODEL), tok(D_MODEL),
            pl.BlockSpec((1, 1, T, PLE_DIM), lambda b, t: (layer, b, t, 0)),
            full((D_MODEL, D_MODEL)), full((1, D_MODEL)), full((D_MODEL, D_MODEL)),
            full((PLE_DIM, D_MODEL)), full((1, D_MODEL)),
        ],
        out_specs=tok(D_MODEL),
        out_shape=jax.ShapeDtypeStruct((B, S, D_MODEL), F32),
        compiler_params=_params(),
        name="post",
    )(o, z, x, p, w_out.astype(BF16), pn.reshape(1, -1), w_gate.astype(BF16), w_proj.astype(BF16),
      fn.reshape(1, -1))


def kernel(x, p, positions, norm_in, mla_w_in, mla_q_norm, mla_kv_norm, mla_w_uq, mla_w_uk, mla_w_uv,
           mla_w_out, dsa_w_in, dsa_w_out, ple_w_proj, ple_norm, ple_w_gate, final_norm):
    cos_t, sin_t, cos_r, sin_r = _rope_tables(positions)
    for i in range(DEPTH):
        j = i // 2
        if i % 2 == 0:
            q_cat, kv_cat, z = _mla_proj(x, cos_r, sin_r, norm_in[i], mla_w_in[j], mla_q_norm[j],
                                         mla_kv_norm[j], mla_w_uq[j], mla_w_uk[j])
            o = _mla_attn(q_cat, kv_cat, mla_w_uv[j])
            w_out = mla_w_out[j]
        else:
            qt, qit, k, ki, vt, wit, z = _dsa_proj(x, cos_t, sin_t, norm_in[i], dsa_w_in[j])
            o = _dsa_attn(qt, qit, wit, k, vt, ki)
            w_out = dsa_w_out[j]
        x = _post(o, z, x, p, i, w_out, ple_norm[i], ple_w_gate[i], ple_w_proj[i], final_norm,
                  final=(i == DEPTH - 1))
    return x
```

```python
import functools

import jax
import jax.numpy as jnp
import numpy as np
from jax import lax
from jax.experimental import pallas as pl
from jax.experimental.pallas import tpu as pltpu

D_MODEL = 1024
DEPTH = 4
PLE_DIM = 256
ROPE_THETA = 10000.0
NORM_EPS = 1e-6

MLA_HEADS = 8
MLA_Q_RANK = 256
MLA_KV_RANK = 256
MLA_NOPE_DIM = 128
MLA_ROPE_DIM = 64
MLA_V_DIM = 128
MLA_WIDTH = MLA_HEADS * MLA_V_DIM
MLA_QK_PAD = MLA_KV_RANK + 128

DSA_HEADS = 8
DSA_KV_HEADS = 2
DSA_GROUP = DSA_HEADS // DSA_KV_HEADS
DSA_HEAD_DIM = 128
DSA_WIDTH = DSA_HEADS * DSA_HEAD_DIM
IDX_HEADS = 8
IDX_DIM = 64
TOPK_MAX = 256
IDX_SCALE = IDX_DIM ** -0.5
assert IDX_SCALE == 0.125

LANES = 128
NEG = -1e30
LOG2E = 1.4426950408889634
VMEM_LIMIT = 56 * 1024 * 1024

ROW_TILE = 256
Q_TILE = 128
K_CHUNK = 512

NT_DIMS = (((1,), (1,)), ((), ()))
BF16 = jnp.bfloat16
F32 = jnp.float32


def _rms(x, g):
    return x * lax.rsqrt(jnp.mean(x * x, axis=-1, keepdims=True) + NORM_EPS) * g


def _params(n_parallel=2):
    return pltpu.CompilerParams(
        dimension_semantics=("arbitrary",) * n_parallel,
        vmem_limit_bytes=VMEM_LIMIT)


def _rope_table_kernel(pos_row_ref, pos_col_ref, inv_col_ref, inv_row_ref, sgn_row_ref,
                       cos_t_ref, sin_t_ref, cos_r_ref, sin_r_ref):
    ang_t = inv_col_ref[...] * pos_row_ref[0]
    cos_t_ref[0] = jnp.cos(ang_t)
    sin_t_ref[0] = jnp.sin(ang_t)
    ang_r = pos_col_ref[0] * inv_row_ref[...]
    cos_r_ref[0] = jnp.cos(ang_r)
    sin_r_ref[0] = jnp.sin(ang_r) * sgn_row_ref[...]


def _rope_tables(positions):
    B, S = positions.shape
    T = 512
    posf = positions.astype(F32)
    inv64 = 1.0 / (ROPE_THETA ** (jnp.arange(64, dtype=F32) / 64))
    inv32 = 1.0 / (ROPE_THETA ** (jnp.arange(32, dtype=F32) / 32))
    inv_col = jnp.concatenate([inv64, inv32]).reshape(96, 1)
    inv_row = jnp.tile(inv32, 4).reshape(1, LANES)
    sgn_row = jnp.tile(jnp.concatenate([-jnp.ones(32, F32), jnp.ones(32, F32)]), 2).reshape(1, LANES)
    full = lambda shape: pl.BlockSpec(shape, lambda b, t: (0,) * len(shape))
    return pl.pallas_call(
        _rope_table_kernel,
        grid=(B, S // T),
        in_specs=[
            pl.BlockSpec((1, 1, T), lambda b, t: (b, 0, t)),
            pl.BlockSpec((1, T, 1), lambda b, t: (b, t, 0)),
            full((96, 1)), full((1, LANES)), full((1, LANES)),
        ],
        out_specs=[
            pl.BlockSpec((1, 96, T), lambda b, t: (b, 0, t)),
            pl.BlockSpec((1, 96, T), lambda b, t: (b, 0, t)),
            pl.BlockSpec((1, T, LANES), lambda b, t: (b, t, 0)),
            pl.BlockSpec((1, T, LANES), lambda b, t: (b, t, 0)),
        ],
        out_shape=[
            jax.ShapeDtypeStruct((B, 96, S), F32),
            jax.ShapeDtypeStruct((B, 96, S), F32),
            jax.ShapeDtypeStruct((B, S, LANES), F32),
            jax.ShapeDtypeStruct((B, S, LANES), F32),
        ],
        compiler_params=_params(),
        name="rope_tables",
    )(posf.reshape(B, 1, S), posf.reshape(B, S, 1), inv_col, inv_row, sgn_row)


def _rope_rows(x, cos_r, sin_r):
    lane = lax.broadcasted_iota(jnp.int32, x.shape, 1)
    first = (lane & 63) < 32
    partner = jnp.where(first, pltpu.roll(x, 96, 1), pltpu.roll(x, 32, 1))
    return x * cos_r + partner * sin_r


def _rope_cols(x, cos_t, sin_t):
    half = x.shape[0] // 2
    x1, x2 = x[:half], x[half:]
    return x1 * cos_t - x2 * sin_t, x2 * cos_t + x1 * sin_t


def _mla_proj_kernel(x_ref, cos_r_ref, sin_r_ref, g_ref, w_in_ref, qn_ref, kvn_ref,
                     w_uq_ref, w_uk_ref, q_ref, kv_ref, z_ref):
    hb = _rms(x_ref[0], g_ref[...]).astype(BF16)
    pr = jnp.dot(hb, w_in_ref[...], preferred_element_type=F32)
    cos_r, sin_r = cos_r_ref[0], sin_r_ref[0]
    c_q = _rms(pr[:, :256], qn_ref[...]).astype(BF16)
    c_kv = _rms(pr[:, 256:512], kvn_ref[...])
    z_ref[0] = pr[:, 512:1536].astype(BF16)
    kv_ref[0, :, :256] = c_kv.astype(BF16)
    kv_ref[0, :, 256:] = _rope_rows(pr[:, 1536:1664], cos_r, sin_r).astype(BF16)
    q = jnp.dot(c_q, w_uq_ref[...], preferred_element_type=F32)
    for h in range(MLA_HEADS):
        q_nope = q[:, h * 128:(h + 1) * 128].astype(BF16)
        q_lat = jnp.dot(q_nope, w_uk_ref[h], preferred_element_type=F32)
        q_ref[0, h, :, :256] = q_lat.astype(BF16)
        q_pe = q[:, 1024 + h * 128:1024 + (h + 1) * 128]
        q_ref[0, h, :, 256:] = _rope_rows(q_pe, cos_r, sin_r).astype(BF16)


def _mla_proj(x, cos_r, sin_r, g, w_in, qn, kvn, w_uq, w_uk):
    B, S, _ = x.shape
    T = ROW_TILE
    w_in_r = jnp.concatenate(
        [w_in[:, :512], w_in[:, 576:], w_in[:, 512:576], jnp.zeros((D_MODEL, 64), w_in.dtype)],
        axis=1).astype(BF16)
    w_uq3 = w_uq.reshape(MLA_Q_RANK, MLA_HEADS, MLA_NOPE_DIM + MLA_ROPE_DIM)
    w_pe = jnp.concatenate([w_uq3[:, :, MLA_NOPE_DIM:], jnp.zeros((MLA_Q_RANK, MLA_HEADS, 64), w_uq.dtype)], axis=2)
    w_uq_r = jnp.concatenate(
        [w_uq3[:, :, :MLA_NOPE_DIM].reshape(MLA_Q_RANK, -1), w_pe.reshape(MLA_Q_RANK, -1)],
        axis=1).astype(BF16)
    w_uk_r = jnp.transpose(w_uk, (1, 2, 0)).astype(BF16)
    full = lambda shape: pl.BlockSpec(shape, lambda b, t: (0,) * len(shape))
    return pl.pallas_call(
        _mla_proj_kernel,
        grid=(B, S // T),
        in_specs=[
            pl.BlockSpec((1, T, D_MODEL), lambda b, t: (b, t, 0)),
            pl.BlockSpec((1, T, LANES), lambda b, t: (b, t, 0)),
            pl.BlockSpec((1, T, LANES), lambda b, t: (b, t, 0)),
            full((1, D_MODEL)), full(w_in_r.shape), full((1, 256)), full((1, 256)),
            full(w_uq_r.shape), full(w_uk_r.shape),
        ],
        out_specs=[
            pl.BlockSpec((1, MLA_HEADS, T, MLA_QK_PAD), lambda b, t: (b, 0, t, 0)),
            pl.BlockSpec((1, T, MLA_QK_PAD), lambda b, t: (b, t, 0)),
            pl.BlockSpec((1, T, MLA_WIDTH), lambda b, t: (b, t, 0)),
        ],
        out_shape=[
            jax.ShapeDtypeStruct((B, MLA_HEADS, S, MLA_QK_PAD), BF16),
            jax.ShapeDtypeStruct((B, S, MLA_QK_PAD), BF16),
            jax.ShapeDtypeStruct((B, S, MLA_WIDTH), BF16),
        ],
        compiler_params=_params(),
        name="mla_proj",
    )(x, cos_r, sin_r, g.reshape(1, -1), w_in_r, qn.reshape(1, -1), kvn.reshape(1, -1), w_uq_r, w_uk_r)


def _lane_fold(x, op):
    out = x[:, :LANES]
    for j in range(1, x.shape[1] // LANES):
        out = op(out, x[:, j * LANES:(j + 1) * LANES])
    return out


def _mla_attn_kernel(q_ref, kv_ref, w_uv_ref, o_ref, t_ref, mpart_ref, lpart_ref, acc_ref):
    qb = pl.program_id(1)
    rows = MLA_HEADS * Q_TILE
    scale2 = (MLA_NOPE_DIM + MLA_ROPE_DIM) ** -0.5 * LOG2E
    q = q_ref[0].reshape(rows, MLA_QK_PAD)
    n_full = qb // (K_CHUNK // Q_TILE)
    mpart_ref[...] = jnp.full(mpart_ref.shape, NEG, F32)

    def chunk_start(c):
        return pl.multiple_of(c * K_CHUNK, K_CHUNK)

    def score_step(c, masked):
        start = chunk_start(c)
        kv = kv_ref[0, pl.ds(start, K_CHUNK), :]
        t = lax.dot_general(q, kv, NT_DIMS, preferred_element_type=F32) * scale2
        if masked:
            kpos = start + lax.broadcasted_iota(jnp.int32, (Q_TILE, K_CHUNK), 1)
            qpos = qb * Q_TILE + lax.broadcasted_iota(jnp.int32, (Q_TILE, K_CHUNK), 0)
            keep = (kpos <= qpos)[None]
            t = jnp.where(keep, t.reshape(MLA_HEADS, Q_TILE, K_CHUNK), NEG).reshape(rows, K_CHUNK)
        t_ref[:, pl.ds(start, K_CHUNK)] = t
        mpart_ref[...] = jnp.maximum(mpart_ref[...], _lane_fold(t, jnp.maximum))

    def score_body(c, carry):
        score_step(c, False)
        return carry

    lax.fori_loop(0, n_full, score_body, 0)
    score_step(n_full, True)

    m = jnp.max(mpart_ref[...], axis=-1, keepdims=True)
    mpart_ref[...] = jnp.broadcast_to(m, (rows, LANES))
    lpart_ref[...] = jnp.zeros(lpart_ref.shape, F32)
    acc_ref[...] = jnp.zeros(acc_ref.shape, F32)

    def pv_body(c, carry):
        start = chunk_start(c)
        t = t_ref[:, pl.ds(start, K_CHUNK)]
        m_b = mpart_ref[...]
        p = jnp.concatenate(
            [jnp.exp2(t[:, j * LANES:(j + 1) * LANES] - m_b) for j in range(K_CHUNK // LANES)], axis=1)
        lpart_ref[...] += _lane_fold(p, jnp.add)
        v = kv_ref[0, pl.ds(start, K_CHUNK), :MLA_KV_RANK]
        acc_ref[...] += jnp.dot(p.astype(BF16), v, preferred_element_type=F32)
        return carry

    lax.fori_loop(0, n_full + 1, pv_body, 0)

    o_lat = acc_ref[...] / jnp.sum(lpart_ref[...], axis=-1, keepdims=True)
    for h in range(MLA_HEADS):
        o_h = jnp.dot(o_lat[h * Q_TILE:(h + 1) * Q_TILE].astype(BF16), w_uv_ref[h],
                      preferred_element_type=F32)
        o_ref[0, :, h * MLA_V_DIM:(h + 1) * MLA_V_DIM] = o_h.astype(BF16)


def _mla_attn(q_cat, kv_cat, w_uv):
    B, H, S, _ = q_cat.shape
    w_uv_r = jnp.transpose(w_uv, (1, 0, 2)).astype(BF16)
    rows = H * Q_TILE
    return pl.pallas_call(
        _mla_attn_kernel,
        grid=(B, S // Q_TILE),
        in_specs=[
            pl.BlockSpec((1, H, Q_TILE, MLA_QK_PAD), lambda b, i: (b, 0, i, 0)),
            pl.BlockSpec((1, S, MLA_QK_PAD), lambda b, i: (b, 0, 0)),
            pl.BlockSpec(w_uv_r.shape, lambda b, i: (0, 0, 0)),
        ],
        out_specs=pl.BlockSpec((1, Q_TILE, MLA_WIDTH), lambda b, i: (b, i, 0)),
        out_shape=jax.ShapeDtypeStruct((B, S, MLA_WIDTH), BF16),
        scratch_shapes=[
            pltpu.VMEM((rows, S), F32),
            pltpu.VMEM((rows, LANES), F32),
            pltpu.VMEM((rows, LANES), F32),
            pltpu.VMEM((rows, MLA_KV_RANK), F32),
        ],
        compiler_params=_params(),
        name="mla_attn",
    )(q_cat, kv_cat, w_uv_r)


DSA_ROW_Q = 0
DSA_ROW_QI = DSA_ROW_Q + DSA_WIDTH
DSA_ROW_K = DSA_ROW_QI + IDX_HEADS * IDX_DIM
DSA_ROW_KI = DSA_ROW_K + DSA_KV_HEADS * DSA_HEAD_DIM
DSA_ROW_V = DSA_ROW_KI + 128
DSA_ROW_WI = DSA_ROW_V + DSA_KV_HEADS * DSA_HEAD_DIM
DSA_ROWS = DSA_ROW_WI + 16


def _dsa_proj_kernel(x_ref, cos_t_ref, sin_t_ref, g_ref, w_f_ref, w_z_ref,
                     qt_ref, qit_ref, k_ref, ki_ref, vt_ref, wit_ref, z_ref):
    hb = _rms(x_ref[0], g_ref[...]).astype(BF16)
    z_ref[0] = jnp.dot(hb, w_z_ref[...], preferred_element_type=F32).astype(BF16)
    pt = lax.dot_general(w_f_ref[...], hb, NT_DIMS, preferred_element_type=F32)
    c64, s64 = cos_t_ref[0, :64], sin_t_ref[0, :64]
    c32, s32 = cos_t_ref[0, 64:96], sin_t_ref[0, 64:96]
    for h in range(DSA_HEADS):
        r = DSA_ROW_Q + h * 128
        o1, o2 = _rope_cols(pt[r:r + 128], c64, s64)
        qt_ref[0, r:r + 64] = o1.astype(BF16)
        qt_ref[0, r + 64:r + 128] = o2.astype(BF16)
    for h in range(IDX_HEADS):
        r = DSA_ROW_QI + h * 64
        o1, o2 = _rope_cols(pt[r:r + 64], c32, s32)
        qit_ref[0, h * 64:h * 64 + 32] = (o1 * IDX_SCALE).astype(BF16)
        qit_ref[0, h * 64 + 32:h * 64 + 64] = (o2 * IDX_SCALE).astype(BF16)
    for g in range(DSA_KV_HEADS):
        r = DSA_ROW_K + g * 128
        o1, o2 = _rope_cols(pt[r:r + 128], c64, s64)
        k_ref[0, g] = jnp.concatenate([o1, o2], axis=0).T.astype(BF16)
    o1, o2 = _rope_cols(pt[DSA_ROW_KI:DSA_ROW_KI + 64], c32, s32)
    ki_pad = jnp.concatenate([o1, o2, pt[DSA_ROW_KI + 64:DSA_ROW_KI + 128]], axis=0)
    ki_ref[0] = ki_pad.T[:, :IDX_DIM].astype(BF16)
    vt_ref[0] = pt[DSA_ROW_V:DSA_ROW_V + 256].astype(BF16)
    wit_ref[0] = pt[DSA_ROW_WI:DSA_ROW_WI + 8] * (IDX_HEADS ** -0.5)


def _dsa_proj(x, cos_t, sin_t, g, w_in):
    B, S, _ = x.shape
    T = ROW_TILE
    c = 0
    q_w = w_in[:, c:c + 1024]; c += 1024
    k_w = w_in[:, c:c + 256]; c += 256
    v_w = w_in[:, c:c + 256]; c += 256
    qi_w = w_in[:, c:c + 512]; c += 512
    ki_w = w_in[:, c:c + 64]; c += 64
    wi_w = w_in[:, c:c + 8]; c += 8
    z_w = w_in[:, c:c + 1024]
    zeros = lambda n: jnp.zeros((D_MODEL, n), w_in.dtype)
    w_f = jnp.concatenate([q_w, qi_w, k_w, ki_w, zeros(64), v_w, wi_w, zeros(8)], axis=1).T.astype(BF16)
    assert w_f.shape == (DSA_ROWS, D_MODEL)
    full = lambda shape: pl.BlockSpec(shape, lambda b, t: (0,) * len(shape))
    return pl.pallas_call(
        _dsa_proj_kernel,
        grid=(B, S // T),
        in_specs=[
            pl.BlockSpec((1, T, D_MODEL), lambda b, t: (b, t, 0)),
            pl.BlockSpec((1, 96, T), lambda b, t: (b, 0, t)),
            pl.BlockSpec((1, 96, T), lambda b, t: (b, 0, t)),
            full((1, D_MODEL)), full(w_f.shape), full((D_MODEL, DSA_WIDTH)),
        ],
        out_specs=[
            pl.BlockSpec((1, DSA_WIDTH, T), lambda b, t: (b, 0, t)),
            pl.BlockSpec((1, IDX_HEADS * IDX_DIM, T), lambda b, t: (b, 0, t)),
            pl.BlockSpec((1, DSA_KV_HEADS, T, DSA_HEAD_DIM), lambda b, t: (b, 0, t, 0)),
            pl.BlockSpec((1, T, IDX_DIM), lambda b, t: (b, t, 0)),
            pl.BlockSpec((1, DSA_KV_HEADS * DSA_HEAD_DIM, T), lambda b, t: (b, 0, t)),
            pl.BlockSpec((1, IDX_HEADS, T), lambda b, t: (b, 0, t)),
            pl.BlockSpec((1, T, DSA_WIDTH), lambda b, t: (b, t, 0)),
        ],
        out_shape=[
            jax.ShapeDtypeStruct((B, DSA_WIDTH, S), BF16),
            jax.ShapeDtypeStruct((B, IDX_HEADS * IDX_DIM, S), BF16),
            jax.ShapeDtypeStruct((B, DSA_KV_HEADS, S, DSA_HEAD_DIM), BF16),
            jax.ShapeDtypeStruct((B, S, IDX_DIM), BF16),
            jax.ShapeDtypeStruct((B, DSA_KV_HEADS * DSA_HEAD_DIM, S), BF16),
            jax.ShapeDtypeStruct((B, IDX_HEADS, S), F32),
            jax.ShapeDtypeStruct((B, S, DSA_WIDTH), BF16),
        ],
        compiler_params=_params(),
        name="dsa_proj",
    )(x, cos_t, sin_t, g.reshape(1, -1), w_f, z_w.astype(BF16))


def _key_to_float(key):
    bits = key ^ ((key >> 31) & jnp.int32(0x7FFFFFFF))
    return lax.bitcast_convert_type(bits, F32)


def _dsa_attn_kernel(qt_ref, qit_ref, wit_ref, k_ref, vt_ref, ki_ref, o_ref,
                     sc_ref, t_ref, acc_ref, *, top_k):
    qb = pl.program_id(1)
    n_chunks = qb // (K_CHUNK // Q_TILE) + 1
    qpos = qb * Q_TILE + lax.broadcasted_iota(jnp.int32, (K_CHUNK, Q_TILE), 1)

    def chunk_start(c):
        return pl.multiple_of(c * K_CHUNK, K_CHUNK)

    qi_all = jnp.concatenate(
        [qit_ref[0, h * IDX_DIM:(h + 1) * IDX_DIM, :] for h in range(IDX_HEADS)], axis=1)
    w_rows = [wit_ref[0, h:h + 1, :] for h in range(IDX_HEADS)]

    def score_body(c, carry):
        start = chunk_start(c)
        ki = ki_ref[0, pl.ds(start, K_CHUNK), :]
        lg = jnp.dot(ki, qi_all, preferred_element_type=F32)
        sc = jnp.zeros((K_CHUNK, Q_TILE), F32)
        for h in range(IDX_HEADS):
            sc = sc + jnp.maximum(lg[:, h * Q_TILE:(h + 1) * Q_TILE], 0.0) * w_rows[h]
        kpos = start + lax.broadcasted_iota(jnp.int32, (K_CHUNK, Q_TILE), 0)
        sc_ref[pl.ds(start, K_CHUNK), :] = jnp.where(kpos <= qpos, sc, -jnp.inf)
        return carry

    lax.fori_loop(0, n_chunks, score_body, 0)

    def count(pred):
        def body(c, cnt):
            blk = sc_ref[pl.ds(chunk_start(c), K_CHUNK), :]
            hit = jnp.where(pred(blk), 1.0, 0.0)
            return cnt + jnp.sum(hit.reshape(8, K_CHUNK // 8, Q_TILE), axis=0)
        cnt = lax.fori_loop(0, n_chunks, body, jnp.zeros((K_CHUNK // 8, Q_TILE), F32))
        return jnp.sum(cnt, axis=0, keepdims=True)

    def search(_):
        def bit_body(i, t):
            cand = t + lax.shift_left(jnp.int32(1), 31 - i)
            cf = _key_to_float(cand)
            n = count(lambda blk: blk >= cf)
            return jnp.where(n >= top_k, cand, t)
        t = lax.fori_loop(0, 32, bit_body, jnp.full((1, Q_TILE), jnp.iinfo(jnp.int32).min, jnp.int32))
        return _key_to_float(t)

    def take_all(_):
        return jnp.full((1, Q_TILE), jnp.finfo(F32).min, F32)

    thr = lax.cond((qb + 1) * Q_TILE > top_k, search, take_all, 0)
    n_ge = count(lambda blk: blk >= thr)
    has_tie = jnp.max(jnp.where(n_ge > top_k, 1.0, 0.0)) > 0.0

    @pl.when(jnp.logical_not(has_tie))
    def _():
        def body(c, carry):
            sl = pl.ds(chunk_start(c), K_CHUNK)
            sc_ref[sl, :] = jnp.where(sc_ref[sl, :] >= thr, 0.0, NEG)
            return carry
        lax.fori_loop(0, n_chunks, body, 0)

    @pl.when(has_tie)
    def _():
        n_gt = count(lambda blk: blk > thr)
        need = top_k - n_gt
        r_i = lax.broadcasted_iota(jnp.int32, (K_CHUNK, K_CHUNK), 0)
        c_i = lax.broadcasted_iota(jnp.int32, (K_CHUNK, K_CHUNK), 1)
        before = jnp.where(c_i < r_i, 1.0, 0.0).astype(BF16)

        def body(c, seen):
            sl = pl.ds(chunk_start(c), K_CHUNK)
            blk = sc_ref[sl, :]
            eq = jnp.where(blk == thr, 1.0, 0.0)
            rank = seen + jnp.dot(before, eq.astype(BF16), preferred_element_type=F32)
            keep = jnp.where(blk > thr, 1.0, jnp.where(rank < need, eq, 0.0))
            sc_ref[sl, :] = jnp.where(keep > 0.0, 0.0, NEG)
            return seen + jnp.sum(eq, axis=0, keepdims=True)
        lax.fori_loop(0, n_chunks, body, jnp.zeros((1, Q_TILE), F32))

    scale2 = DSA_HEAD_DIM ** -0.5 * LOG2E
    cols = DSA_GROUP * Q_TILE
    q_grp = [
        jnp.concatenate(
            [qt_ref[0, (g * DSA_GROUP + j) * DSA_HEAD_DIM:(g * DSA_GROUP + j + 1) * DSA_HEAD_DIM, :]
             for j in range(DSA_GROUP)], axis=1)
        for g in range(DSA_KV_HEADS)]

    def fold_rows(x, op):
        a = op(x.reshape(8, K_CHUNK // 8, x.shape[1]), axis=0)
        return op(a.reshape(8, 8, x.shape[1]), axis=0)

    def score_body(c, m8):
        start = chunk_start(c)
        bias = sc_ref[pl.ds(start, K_CHUNK), :]
        out = []
        for g in range(DSA_KV_HEADS):
            k = k_ref[0, g, pl.ds(start, K_CHUNK), :]
            s = jnp.dot(k, q_grp[g], preferred_element_type=F32)
            col_max = []
            for j in range(DSA_GROUP):
                t = s[:, j * Q_TILE:(j + 1) * Q_TILE] * scale2 + bias
                t_ref[g, pl.ds(start, K_CHUNK), j * Q_TILE:(j + 1) * Q_TILE] = t
                col_max.append(fold_rows(t, jnp.max))
            out.append(jnp.maximum(m8[g], jnp.concatenate(col_max, axis=1)))
        return tuple(out)

    m8 = lax.fori_loop(0, n_chunks, score_body,
                       tuple(jnp.full((8, cols), NEG, F32) for _ in range(DSA_KV_HEADS)))
    m_row = [jnp.max(m, axis=0, keepdims=True) for m in m8]
    acc_ref[...] = jnp.zeros(acc_ref.shape, F32)

    def pv_body(c, l8):
        start = chunk_start(c)
        out = []
        for g in range(DSA_KV_HEADS):
            p = jnp.exp2(t_ref[g, pl.ds(start, K_CHUNK), :] - m_row[g])
            vt = vt_ref[0, g * DSA_HEAD_DIM:(g + 1) * DSA_HEAD_DIM, pl.ds(start, K_CHUNK)]
            acc_ref[g] += jnp.dot(vt, p.astype(BF16), preferred_element_type=F32)
            out.append(l8[g] + fold_rows(p, jnp.sum))
        return tuple(out)

    l8 = lax.fori_loop(0, n_chunks, pv_body,
                       tuple(jnp.zeros((8, cols), F32) for _ in range(DSA_KV_HEADS)))
    for g in range(DSA_KV_HEADS):
        o_t = acc_ref[g] / jnp.sum(l8[g], axis=0, keepdims=True)
        for j in range(DSA_GROUP):
            h = g * DSA_GROUP + j
            o_ref[0, :, h * DSA_HEAD_DIM:(h + 1) * DSA_HEAD_DIM] = (
                o_t[:, j * Q_TILE:(j + 1) * Q_TILE].T.astype(BF16))


def _dsa_attn(qt, qit, wit, k, vt, ki):
    B, _, S = qt.shape
    top_k = min(TOPK_MAX, S // 4)
    cols = DSA_GROUP * Q_TILE
    return pl.pallas_call(
        functools.partial(_dsa_attn_kernel, top_k=top_k),
        grid=(B, S // Q_TILE),
        in_specs=[
            pl.BlockSpec((1, DSA_WIDTH, Q_TILE), lambda b, i: (b, 0, i)),
            pl.BlockSpec((1, IDX_HEADS * IDX_DIM, Q_TILE), lambda b, i: (b, 0, i)),
            pl.BlockSpec((1, IDX_HEADS, Q_TILE), lambda b, i: (b, 0, i)),
            pl.BlockSpec((1, DSA_KV_HEADS, S, DSA_HEAD_DIM), lambda b, i: (b, 0, 0, 0)),
            pl.BlockSpec((1, DSA_KV_HEADS * DSA_HEAD_DIM, S), lambda b, i: (b, 0, 0)),
            pl.BlockSpec((1, S, IDX_DIM), lambda b, i: (b, 0, 0)),
        ],
        out_specs=pl.BlockSpec((1, Q_TILE, DSA_WIDTH), lambda b, i: (b, i, 0)),
        out_shape=jax.ShapeDtypeStruct((B, S, DSA_WIDTH), BF16),
        scratch_shapes=[
            pltpu.VMEM((S, Q_TILE), F32),
            pltpu.VMEM((DSA_KV_HEADS, S, cols), F32),
            pltpu.VMEM((DSA_KV_HEADS, DSA_HEAD_DIM, cols), F32),
        ],
        compiler_params=_params(),
        name="dsa_attn",
    )(qt, qit, wit, k, vt, ki)


def _post_kernel(o_ref, z_ref, x_ref, p_ref, w_out_ref, pn_ref, w_gate_ref, w_proj_ref, fn_ref,
                 y_ref, *, final):
    z = z_ref[0].astype(F32)
    gated = o_ref[0].astype(F32) * (z / (1.0 + jnp.exp(-z)))
    x1 = x_ref[0] + jnp.dot(gated.astype(BF16), w_out_ref[...], preferred_element_type=F32)
    pre = jnp.dot(_rms(x1, pn_ref[...]).astype(BF16), w_gate_ref[...], preferred_element_type=F32)
    gate = 1.0 / (1.0 + jnp.exp(-pre))
    emb = jnp.dot(p_ref[0, 0].astype(BF16), w_proj_ref[...], preferred_element_type=F32)
    x2 = x1 + emb * gate
    y_ref[0] = _rms(x2, fn_ref[...]) if final else x2


def _post(o, z, x, p, layer, w_out, pn, w_gate, w_proj, fn, final):
    B, S, _ = x.shape
    T = ROW_TILE
    full = lambda shape: pl.BlockSpec(shape, lambda b, t: (0,) * len(shape))
    tok = lambda d: pl.BlockSpec((1, T, d), lambda b, t: (b, t, 0))
    return pl.pallas_call(
        functools.partial(_post_kernel, final=final),
        grid=(B, S // T),
        in_specs=[
            tok(D_MODEL), tok(D_MODEL), tok(D_MODEL),
            pl.BlockSpec((1, 1, T, PLE_DIM), lambda b, t: (layer, b, t, 0)),
            full((D_MODEL, D_MODEL)), full((1, D_MODEL)), full((D_MODEL, D_MODEL)),
            full((PLE_DIM, D_MODEL)), full((1, D_MODEL)),
        ],
        out_specs=tok(D_MODEL),
        out_shape=jax.ShapeDtypeStruct((B, S, D_MODEL), F32),
        compiler_params=_params(),
        name="post",
    )(o, z, x, p, w_out.astype(BF16), pn.reshape(1, -1), w_gate.astype(BF16), w_proj.astype(BF16),
      fn.reshape(1, -1))


def kernel(x, p, positions, norm_in, mla_w_in, mla_q_norm, mla_kv_norm, mla_w_uq, mla_w_uk, mla_w_uv,
           mla_w_out, dsa_w_in, dsa_w_out, ple_w_proj, ple_norm, ple_w_gate, final_norm):
    cos_t, sin_t, cos_r, sin_r = _rope_tables(positions)
    for i in range(DEPTH):
        j = i // 2
        if i % 2 == 0:
            q_cat, kv_cat, z = _mla_proj(x, cos_r, sin_r, norm_in[i], mla_w_in[j], mla_q_norm[j],
                                         mla_kv_norm[j], mla_w_uq[j], mla_w_uk[j])
            o = _mla_attn(q_cat, kv_cat, mla_w_uv[j])
            w_out = mla_w_out[j]
        else:
            qt, qit, k, ki, vt, wit, z = _dsa_proj(x, cos_t, sin_t, norm_in[i], dsa_w_in[j])
            o = _dsa_attn(qt, qit, wit, k, vt, ki)
            w_out = dsa_w_out[j]
        x = _post(o, z, x, p, i, w_out, ple_norm[i], ple_w_gate[i], ple_w_proj[i], final_norm,
                  final=(i == DEPTH - 1))
    return x
```

```python
import functools

import jax
import jax.numpy as jnp
import numpy as np
from jax import lax
from jax.experimental import pallas as pl
from jax.experimental.pallas import tpu as pltpu

D_MODEL = 1024
DEPTH = 4
PLE_DIM = 256
ROPE_THETA = 10000.0
NORM_EPS = 1e-6

MLA_HEADS = 8
MLA_Q_RANK = 256
MLA_KV_RANK = 256
MLA_NOPE_DIM = 128
MLA_ROPE_DIM = 64
MLA_V_DIM = 128
MLA_WIDTH = MLA_HEADS * MLA_V_DIM
MLA_QK_PAD = MLA_KV_RANK + 128

DSA_HEADS = 8
DSA_KV_HEADS = 2
DSA_GROUP = DSA_HEADS // DSA_KV_HEADS
DSA_HEAD_DIM = 128
DSA_WIDTH = DSA_HEADS * DSA_HEAD_DIM
IDX_HEADS = 8
IDX_DIM = 64
TOPK_MAX = 256
IDX_SCALE = IDX_DIM ** -0.5
assert IDX_SCALE == 0.125

LANES = 128
NEG = -1e30
LOG2E = 1.4426950408889634
VMEM_LIMIT = 56 * 1024 * 1024

ROW_TILE = 512
Q_TILE = 128
K_CHUNK = 512

NT_DIMS = (((1,), (1,)), ((), ()))
BF16 = jnp.bfloat16
F32 = jnp.float32


def _rms(x, g):
    return x * lax.rsqrt(jnp.mean(x * x, axis=-1, keepdims=True) + NORM_EPS) * g


def _params(n_parallel=2):
    return pltpu.CompilerParams(
        dimension_semantics=("arbitrary",) * n_parallel,
        vmem_limit_bytes=VMEM_LIMIT)


def _rope_table_kernel(pos_row_ref, pos_col_ref, inv_col_ref, inv_row_ref, sgn_row_ref,
                       cos_t_ref, sin_t_ref, cos_r_ref, sin_r_ref):
    ang_t = inv_col_ref[...] * pos_row_ref[0]
    cos_t_ref[0] = jnp.cos(ang_t)
    sin_t_ref[0] = jnp.sin(ang_t)
    ang_r = pos_col_ref[0] * inv_row_ref[...]
    cos_r_ref[0] = jnp.cos(ang_r)
    sin_r_ref[0] = jnp.sin(ang_r) * sgn_row_ref[...]


def _rope_tables(positions):
    B, S = positions.shape
    T = 512
    posf = positions.astype(F32)
    inv64 = 1.0 / (ROPE_THETA ** (jnp.arange(64, dtype=F32) / 64))
    inv32 = 1.0 / (ROPE_THETA ** (jnp.arange(32, dtype=F32) / 32))
    inv_col = jnp.concatenate([inv64, inv32]).reshape(96, 1)
    inv_row = jnp.tile(inv32, 4).reshape(1, LANES)
    sgn_row = jnp.tile(jnp.concatenate([-jnp.ones(32, F32), jnp.ones(32, F32)]), 2).reshape(1, LANES)
    full = lambda shape: pl.BlockSpec(shape, lambda b, t: (0,) * len(shape))
    return pl.pallas_call(
        _rope_table_kernel,
        grid=(B, S // T),
        in_specs=[
            pl.BlockSpec((1, 1, T), lambda b, t: (b, 0, t)),
            pl.BlockSpec((1, T, 1), lambda b, t: (b, t, 0)),
            full((96, 1)), full((1, LANES)), full((1, LANES)),
        ],
        out_specs=[
            pl.BlockSpec((1, 96, T), lambda b, t: (b, 0, t)),
            pl.BlockSpec((1, 96, T), lambda b, t: (b, 0, t)),
            pl.BlockSpec((1, T, LANES), lambda b, t: (b, t, 0)),
            pl.BlockSpec((1, T, LANES), lambda b, t: (b, t, 0)),
        ],
        out_shape=[
            jax.ShapeDtypeStruct((B, 96, S), F32),
            jax.ShapeDtypeStruct((B, 96, S), F32),
            jax.ShapeDtypeStruct((B, S, LANES), F32),
            jax.ShapeDtypeStruct((B, S, LANES), F32),
        ],
        compiler_params=_params(),
        name="rope_tables",
    )(posf.reshape(B, 1, S), posf.reshape(B, S, 1), inv_col, inv_row, sgn_row)


def _rope_rows(x, cos_r, sin_r):
    lane = lax.broadcasted_iota(jnp.int32, x.shape, 1)
    first = (lane & 63) < 32
    partner = jnp.where(first, pltpu.roll(x, 96, 1), pltpu.roll(x, 32, 1))
    return x * cos_r + partner * sin_r


def _rope_cols(x, cos_t, sin_t):
    half = x.shape[0] // 2
    x1, x2 = x[:half], x[half:]
    return x1 * cos_t - x2 * sin_t, x2 * cos_t + x1 * sin_t


def _mla_proj_kernel(x_ref, cos_r_ref, sin_r_ref, g_ref, w_in_ref, qn_ref, kvn_ref,
                     w_uq_ref, w_uk_ref, q_ref, kv_ref, z_ref):
    hb = _rms(x_ref[0], g_ref[...]).astype(BF16)
    pr = jnp.dot(hb, w_in_ref[...], preferred_element_type=F32)
    cos_r, sin_r = cos_r_ref[0], sin_r_ref[0]
    c_q = _rms(pr[:, :256], qn_ref[...]).astype(BF16)
    c_kv = _rms(pr[:, 256:512], kvn_ref[...])
    z_ref[0] = pr[:, 512:1536].astype(BF16)
    kv_ref[0, :, :256] = c_kv.astype(BF16)
    kv_ref[0, :, 256:] = _rope_rows(pr[:, 1536:1664], cos_r, sin_r).astype(BF16)
    q = jnp.dot(c_q, w_uq_ref[...], preferred_element_type=F32)
    for h in range(MLA_HEADS):
        q_nope = q[:, h * 128:(h + 1) * 128].astype(BF16)
        q_lat = jnp.dot(q_nope, w_uk_ref[h], preferred_element_type=F32)
        q_ref[0, h, :, :256] = q_lat.astype(BF16)
        q_pe = q[:, 1024 + h * 128:1024 + (h + 1) * 128]
        q_ref[0, h, :, 256:] = _rope_rows(q_pe, cos_r, sin_r).astype(BF16)


def _mla_proj(x, cos_r, sin_r, g, w_in, qn, kvn, w_uq, w_uk):
    B, S, _ = x.shape
    T = ROW_TILE
    w_in_r = jnp.concatenate(
        [w_in[:, :512], w_in[:, 576:], w_in[:, 512:576], jnp.zeros((D_MODEL, 64), w_in.dtype)],
        axis=1).astype(BF16)
    w_uq3 = w_uq.reshape(MLA_Q_RANK, MLA_HEADS, MLA_NOPE_DIM + MLA_ROPE_DIM)
    w_pe = jnp.concatenate([w_uq3[:, :, MLA_NOPE_DIM:], jnp.zeros((MLA_Q_RANK, MLA_HEADS, 64), w_uq.dtype)], axis=2)
    w_uq_r = jnp.concatenate(
        [w_uq3[:, :, :MLA_NOPE_DIM].reshape(MLA_Q_RANK, -1), w_pe.reshape(MLA_Q_RANK, -1)],
        axis=1).astype(BF16)
    w_uk_r = jnp.transpose(w_uk, (1, 2, 0)).astype(BF16)
    full = lambda shape: pl.BlockSpec(shape, lambda b, t: (0,) * len(shape))
    return pl.pallas_call(
        _mla_proj_kernel,
        grid=(B, S // T),
        in_specs=[
            pl.BlockSpec((1, T, D_MODEL), lambda b, t: (b, t, 0)),
            pl.BlockSpec((1, T, LANES), lambda b, t: (b, t, 0)),
            pl.BlockSpec((1, T, LANES), lambda b, t: (b, t, 0)),
            full((1, D_MODEL)), full(w_in_r.shape), full((1, 256)), full((1, 256)),
            full(w_uq_r.shape), full(w_uk_r.shape),
        ],
        out_specs=[
            pl.BlockSpec((1, MLA_HEADS, T, MLA_QK_PAD), lambda b, t: (b, 0, t, 0)),
            pl.BlockSpec((1, T, MLA_QK_PAD), lambda b, t: (b, t, 0)),
            pl.BlockSpec((1, T, MLA_WIDTH), lambda b, t: (b, t, 0)),
        ],
        out_shape=[
            jax.ShapeDtypeStruct((B, MLA_HEADS, S, MLA_QK_PAD), BF16),
            jax.ShapeDtypeStruct((B, S, MLA_QK_PAD), BF16),
            jax.ShapeDtypeStruct((B, S, MLA_WIDTH), BF16),
        ],
        compiler_params=_params(),
        name="mla_proj",
    )(x, cos_r, sin_r, g.reshape(1, -1), w_in_r, qn.reshape(1, -1), kvn.reshape(1, -1), w_uq_r, w_uk_r)


def _lane_fold(x, op):
    out = x[:, :LANES]
    for j in range(1, x.shape[1] // LANES):
        out = op(out, x[:, j * LANES:(j + 1) * LANES])
    return out


def _mla_attn_kernel(q_ref, kv_ref, w_uv_ref, o_ref, t_ref, mpart_ref, lpart_ref, acc_ref):
    qb = pl.program_id(1)
    rows = MLA_HEADS * Q_TILE
    scale2 = (MLA_NOPE_DIM + MLA_ROPE_DIM) ** -0.5 * LOG2E
    q = q_ref[0].reshape(rows, MLA_QK_PAD)
    n_full = qb // (K_CHUNK // Q_TILE)
    mpart_ref[...] = jnp.full(mpart_ref.shape, NEG, F32)

    def chunk_start(c):
        return pl.multiple_of(c * K_CHUNK, K_CHUNK)

    def score_step(c, masked):
        start = chunk_start(c)
        kv = kv_ref[0, pl.ds(start, K_CHUNK), :]
        t = lax.dot_general(q, kv, NT_DIMS, preferred_element_type=F32) * scale2
        if masked:
            kpos = start + lax.broadcasted_iota(jnp.int32, (Q_TILE, K_CHUNK), 1)
            qpos = qb * Q_TILE + lax.broadcasted_iota(jnp.int32, (Q_TILE, K_CHUNK), 0)
            keep = (kpos <= qpos)[None]
            t = jnp.where(keep, t.reshape(MLA_HEADS, Q_TILE, K_CHUNK), NEG).reshape(rows, K_CHUNK)
        t_ref[:, pl.ds(start, K_CHUNK)] = t
        mpart_ref[...] = jnp.maximum(mpart_ref[...], _lane_fold(t, jnp.maximum))

    def score_body(c, carry):
        score_step(c, False)
        return carry

    lax.fori_loop(0, n_full, score_body, 0)
    score_step(n_full, True)

    m = jnp.max(mpart_ref[...], axis=-1, keepdims=True)
    mpart_ref[...] = jnp.broadcast_to(m, (rows, LANES))
    lpart_ref[...] = jnp.zeros(lpart_ref.shape, F32)
    acc_ref[...] = jnp.zeros(acc_ref.shape, F32)

    def pv_body(c, carry):
        start = chunk_start(c)
        t = t_ref[:, pl.ds(start, K_CHUNK)]
        m_b = mpart_ref[...]
        p = jnp.concatenate(
            [jnp.exp2(t[:, j * LANES:(j + 1) * LANES] - m_b) for j in range(K_CHUNK // LANES)], axis=1)
        lpart_ref[...] += _lane_fold(p, jnp.add)
        v = kv_ref[0, pl.ds(start, K_CHUNK), :MLA_KV_RANK]
        acc_ref[...] += jnp.dot(p.astype(BF16), v, preferred_element_type=F32)
        return carry

    lax.fori_loop(0, n_full + 1, pv_body, 0)

    o_lat = acc_ref[...] / jnp.sum(lpart_ref[...], axis=-1, keepdims=True)
    for h in range(MLA_HEADS):
        o_h = jnp.dot(o_lat[h * Q_TILE:(h + 1) * Q_TILE].astype(BF16), w_uv_ref[h],
                      preferred_element_type=F32)
        o_ref[0, :, h * MLA_V_DIM:(h + 1) * MLA_V_DIM] = o_h.astype(BF16)


def _mla_attn(q_cat, kv_cat, w_uv):
    B, H, S, _ = q_cat.shape
    w_uv_r = jnp.transpose(w_uv, (1, 0, 2)).astype(BF16)
    rows = H * Q_TILE
    return pl.pallas_call(
        _mla_attn_kernel,
        grid=(B, S // Q_TILE),
        in_specs=[
            pl.BlockSpec((1, H, Q_TILE, MLA_QK_PAD), lambda b, i: (b, 0, i, 0)),
            pl.BlockSpec((1, S, MLA_QK_PAD), lambda b, i: (b, 0, 0)),
            pl.BlockSpec(w_uv_r.shape, lambda b, i: (0, 0, 0)),
        ],
        out_specs=pl.BlockSpec((1, Q_TILE, MLA_WIDTH), lambda b, i: (b, i, 0)),
        out_shape=jax.ShapeDtypeStruct((B, S, MLA_WIDTH), BF16),
        scratch_shapes=[
            pltpu.VMEM((rows, S), F32),
            pltpu.VMEM((rows, LANES), F32),
            pltpu.VMEM((rows, LANES), F32),
            pltpu.VMEM((rows, MLA_KV_RANK), F32),
        ],
        compiler_params=_params(),
        name="mla_attn",
    )(q_cat, kv_cat, w_uv_r)


DSA_ROW_Q = 0
DSA_ROW_QI = DSA_ROW_Q + DSA_WIDTH
DSA_ROW_K = DSA_ROW_QI + IDX_HEADS * IDX_DIM
DSA_ROW_KI = DSA_ROW_K + DSA_KV_HEADS * DSA_HEAD_DIM
DSA_ROW_V = DSA_ROW_KI + 128
DSA_ROW_WI = DSA_ROW_V + DSA_KV_HEADS * DSA_HEAD_DIM
DSA_ROWS = DSA_ROW_WI + 16


def _dsa_proj_kernel(x_ref, cos_t_ref, sin_t_ref, g_ref, w_f_ref, w_z_ref,
                     qt_ref, qit_ref, k_ref, ki_ref, vt_ref, wit_ref, z_ref):
    hb = _rms(x_ref[0], g_ref[...]).astype(BF16)
    z_ref[0] = jnp.dot(hb, w_z_ref[...], preferred_element_type=F32).astype(BF16)
    pt = lax.dot_general(w_f_ref[...], hb, NT_DIMS, preferred_element_type=F32)
    c64, s64 = cos_t_ref[0, :64], sin_t_ref[0, :64]
    c32, s32 = cos_t_ref[0, 64:96], sin_t_ref[0, 64:96]
    for h in range(DSA_HEADS):
        r = DSA_ROW_Q + h * 128
        o1, o2 = _rope_cols(pt[r:r + 128], c64, s64)
        qt_ref[0, r:r + 64] = o1.astype(BF16)
        qt_ref[0, r + 64:r + 128] = o2.astype(BF16)
    for h in range(IDX_HEADS):
        r = DSA_ROW_QI + h * 64
        o1, o2 = _rope_cols(pt[r:r + 64], c32, s32)
        qit_ref[0, h * 64:h * 64 + 32] = (o1 * IDX_SCALE).astype(BF16)
        qit_ref[0, h * 64 + 32:h * 64 + 64] = (o2 * IDX_SCALE).astype(BF16)
    for g in range(DSA_KV_HEADS):
        r = DSA_ROW_K + g * 128
        o1, o2 = _rope_cols(pt[r:r + 128], c64, s64)
        k_ref[0, g] = jnp.concatenate([o1, o2], axis=0).T.astype(BF16)
    o1, o2 = _rope_cols(pt[DSA_ROW_KI:DSA_ROW_KI + 64], c32, s32)
    ki_pad = jnp.concatenate([o1, o2, pt[DSA_ROW_KI + 64:DSA_ROW_KI + 128]], axis=0)
    ki_ref[0] = ki_pad.T[:, :IDX_DIM].astype(BF16)
    vt_ref[0] = pt[DSA_ROW_V:DSA_ROW_V + 256].astype(BF16)
    wit_ref[0] = pt[DSA_ROW_WI:DSA_ROW_WI + 8] * (IDX_HEADS ** -0.5)


def _dsa_proj(x, cos_t, sin_t, g, w_in):
    B, S, _ = x.shape
    T = ROW_TILE
    c = 0
    q_w = w_in[:, c:c + 1024]; c += 1024
    k_w = w_in[:, c:c + 256]; c += 256
    v_w = w_in[:, c:c + 256]; c += 256
    qi_w = w_in[:, c:c + 512]; c += 512
    ki_w = w_in[:, c:c + 64]; c += 64
    wi_w = w_in[:, c:c + 8]; c += 8
    z_w = w_in[:, c:c + 1024]
    zeros = lambda n: jnp.zeros((D_MODEL, n), w_in.dtype)
    w_f = jnp.concatenate([q_w, qi_w, k_w, ki_w, zeros(64), v_w, wi_w, zeros(8)], axis=1).T.astype(BF16)
    assert w_f.shape == (DSA_ROWS, D_MODEL)
    full = lambda shape: pl.BlockSpec(shape, lambda b, t: (0,) * len(shape))
    return pl.pallas_call(
        _dsa_proj_kernel,
        grid=(B, S // T),
        in_specs=[
            pl.BlockSpec((1, T, D_MODEL), lambda b, t: (b, t, 0)),
            pl.BlockSpec((1, 96, T), lambda b, t: (b, 0, t)),
            pl.BlockSpec((1, 96, T), lambda b, t: (b, 0, t)),
            full((1, D_MODEL)), full(w_f.shape), full((D_MODEL, DSA_WIDTH)),
        ],
        out_specs=[
            pl.BlockSpec((1, DSA_WIDTH, T), lambda b, t: (b, 0, t)),
            pl.BlockSpec((1, IDX_HEADS * IDX_DIM, T), lambda b, t: (b, 0, t)),
            pl.BlockSpec((1, DSA_KV_HEADS, T, DSA_HEAD_DIM), lambda b, t: (b, 0, t, 0)),
            pl.BlockSpec((1, T, IDX_DIM), lambda b, t: (b, t, 0)),
            pl.BlockSpec((1, DSA_KV_HEADS * DSA_HEAD_DIM, T), lambda b, t: (b, 0, t)),
            pl.BlockSpec((1, IDX_HEADS, T), lambda b, t: (b, 0, t)),
            pl.BlockSpec((1, T, DSA_WIDTH), lambda b, t: (b, t, 0)),
        ],
        out_shape=[
            jax.ShapeDtypeStruct((B, DSA_WIDTH, S), BF16),
            jax.ShapeDtypeStruct((B, IDX_HEADS * IDX_DIM, S), BF16),
            jax.ShapeDtypeStruct((B, DSA_KV_HEADS, S, DSA_HEAD_DIM), BF16),
            jax.ShapeDtypeStruct((B, S, IDX_DIM), BF16),
            jax.ShapeDtypeStruct((B, DSA_KV_HEADS * DSA_HEAD_DIM, S), BF16),
            jax.ShapeDtypeStruct((B, IDX_HEADS, S), F32),
            jax.ShapeDtypeStruct((B, S, DSA_WIDTH), BF16),
        ],
        compiler_params=_params(),
        name="dsa_proj",
    )(x, cos_t, sin_t, g.reshape(1, -1), w_f, z_w.astype(BF16))


INT_MIN = np.int32(-2 ** 31)
WORD = 32
PLANE_ROWS = K_CHUNK // WORD


def _float_to_ukey(x):
    bits = lax.bitcast_convert_type(x, jnp.int32)
    return bits ^ ((bits >> 31) | INT_MIN)


def _ukey_to_float(u):
    bits = jnp.where(u < 0, u ^ INT_MIN, ~u)
    return lax.bitcast_convert_type(bits, F32)


def _bit_transpose32(words):
    a = list(words)
    j, m = 16, 0x0000FFFF
    while j:
        mask = np.uint32(m).astype(np.int32)
        k = 0
        while k < 32:
            t = (a[k] ^ lax.shift_right_logical(a[k + j], jnp.int32(j))) & mask
            a[k] = a[k] ^ t
            a[k + j] = a[k + j] ^ lax.shift_left(t, jnp.int32(j))
            k = (k + j + 1) & ~j
        j >>= 1
        m = (m ^ (m << j)) & 0xFFFFFFFF
    return a


def _dsa_attn_kernel(qt_ref, qit_ref, wit_ref, k_ref, vt_ref, ki_ref, o_ref,
                     sc_ref, planes_ref, t_ref, acc_ref, *, top_k):
    qb = pl.program_id(1)
    n_chunks = qb // (K_CHUNK // Q_TILE) + 1
    qpos = qb * Q_TILE + lax.broadcasted_iota(jnp.int32, (K_CHUNK, Q_TILE), 1)

    def chunk_start(c):
        return pl.multiple_of(c * K_CHUNK, K_CHUNK)

    qi_all = jnp.concatenate(
        [qit_ref[0, h * IDX_DIM:(h + 1) * IDX_DIM, :] for h in range(IDX_HEADS)], axis=1)
    w_rows = [wit_ref[0, h:h + 1, :] for h in range(IDX_HEADS)]

    def score_body(c, carry):
        start = chunk_start(c)
        ki = ki_ref[0, pl.ds(start, K_CHUNK), :]
        lg = jnp.dot(ki, qi_all, preferred_element_type=F32)
        sc = jnp.zeros((K_CHUNK, Q_TILE), F32)
        for h in range(IDX_HEADS):
            sc = sc + jnp.maximum(lg[:, h * Q_TILE:(h + 1) * Q_TILE], 0.0) * w_rows[h]
        kpos = start + lax.broadcasted_iota(jnp.int32, (K_CHUNK, Q_TILE), 0)
        sc = jnp.where(kpos <= qpos, sc, -jnp.inf)
        sc_ref[pl.ds(start, K_CHUNK), :] = sc
        u = _float_to_ukey(sc)
        for half in range(K_CHUNK // (8 * WORD)):
            base = half * 8 * WORD
            planes = _bit_transpose32([u[base + 8 * v:base + 8 * v + 8] for v in range(WORD)])
            row = pl.multiple_of(c * PLANE_ROWS + half * 8, 8)
            for i in range(WORD):
                planes_ref[i, pl.ds(row, 8), :] = planes[i]
        return carry

    lax.fori_loop(0, n_chunks, score_body, 0)

    n_rows = planes_ref.shape[1]

    def lane_count(words):
        pc = lax.population_count(words).astype(F32)
        a = jnp.sum(pc.reshape(4, n_rows // 4, Q_TILE), axis=0)
        a = jnp.sum(a.reshape(n_rows // 32, 8, Q_TILE), axis=0)
        return jnp.sum(a, axis=0, keepdims=True)

    def search(_):
        def zero_fill(c, carry):
            row = pl.multiple_of(c * PLANE_ROWS, PLANE_ROWS)
            planes_ref[:, pl.ds(row, PLANE_ROWS), :] = jnp.zeros((WORD, PLANE_ROWS, Q_TILE), jnp.int32)
            return carry
        lax.fori_loop(n_chunks, n_rows // PLANE_ROWS, zero_fill, 0)

        row_id = lax.broadcasted_iota(jnp.int32, (n_rows, Q_TILE), 0)
        alive0 = jnp.where(row_id < n_chunks * PLANE_ROWS, jnp.int32(-1), jnp.int32(0))

        def bit_body(i, carry):
            alive, rem, t = carry
            ones = alive & planes_ref[i]
            n = lane_count(ones)
            take = n >= rem
            alive = jnp.where(take, ones, alive ^ ones)
            rem = jnp.where(take, rem, rem - n)
            t = jnp.where(take, t | lax.shift_left(jnp.int32(1), 31 - i), t)
            return alive, rem, t

        alive, rem, t = lax.fori_loop(
            0, 32, bit_body,
            (alive0, jnp.full((1, Q_TILE), float(top_k), F32), jnp.zeros((1, Q_TILE), jnp.int32)))
        return _ukey_to_float(t), rem, lane_count(alive) - rem

    def take_all(_):
        return (jnp.full((1, Q_TILE), jnp.finfo(F32).min, F32), jnp.zeros((1, Q_TILE), F32),
                jnp.zeros((1, Q_TILE), F32))

    thr, need, surplus = lax.cond((qb + 1) * Q_TILE > top_k, search, take_all, 0)
    has_tie = jnp.max(surplus) > 0.0

    @pl.when(jnp.logical_not(has_tie))
    def _():
        def body(c, carry):
            sl = pl.ds(chunk_start(c), K_CHUNK)
            sc_ref[sl, :] = jnp.where(sc_ref[sl, :] >= thr, 0.0, NEG)
            return carry
        lax.fori_loop(0, n_chunks, body, 0)

    @pl.when(has_tie)
    def _():
        r_i = lax.broadcasted_iota(jnp.int32, (K_CHUNK, K_CHUNK), 0)
        c_i = lax.broadcasted_iota(jnp.int32, (K_CHUNK, K_CHUNK), 1)
        before = jnp.where(c_i < r_i, 1.0, 0.0).astype(BF16)

        def body(c, seen):
            sl = pl.ds(chunk_start(c), K_CHUNK)
            blk = sc_ref[sl, :]
            eq = jnp.where(blk == thr, 1.0, 0.0)
            rank = seen + jnp.dot(before, eq.astype(BF16), preferred_element_type=F32)
            keep = jnp.where(blk > thr, 1.0, jnp.where(rank < need, eq, 0.0))
            sc_ref[sl, :] = jnp.where(keep > 0.0, 0.0, NEG)
            return seen + jnp.sum(eq, axis=0, keepdims=True)
        lax.fori_loop(0, n_chunks, body, jnp.zeros((1, Q_TILE), F32))

    scale2 = DSA_HEAD_DIM ** -0.5 * LOG2E
    cols = DSA_GROUP * Q_TILE
    q_grp = [
        jnp.concatenate(
            [qt_ref[0, (g * DSA_GROUP + j) * DSA_HEAD_DIM:(g * DSA_GROUP + j + 1) * DSA_HEAD_DIM, :]
             for j in range(DSA_GROUP)], axis=1)
        for g in range(DSA_KV_HEADS)]

    def fold_rows(x, op):
        a = op(x.reshape(8, K_CHUNK // 8, x.shape[1]), axis=0)
        return op(a.reshape(8, 8, x.shape[1]), axis=0)

    def score_body(c, m8):
        start = chunk_start(c)
        bias = sc_ref[pl.ds(start, K_CHUNK), :]
        out = []
        for g in range(DSA_KV_HEADS):
            k = k_ref[0, g, pl.ds(start, K_CHUNK), :]
            s = jnp.dot(k, q_grp[g], preferred_element_type=F32)
            col_max = []
            for j in range(DSA_GROUP):
                t = s[:, j * Q_TILE:(j + 1) * Q_TILE] * scale2 + bias
                t_ref[g, pl.ds(start, K_CHUNK), j * Q_TILE:(j + 1) * Q_TILE] = t
                col_max.append(fold_rows(t, jnp.max))
            out.append(jnp.maximum(m8[g], jnp.concatenate(col_max, axis=1)))
        return tuple(out)

    m8 = lax.fori_loop(0, n_chunks, score_body,
                       tuple(jnp.full((8, cols), NEG, F32) for _ in range(DSA_KV_HEADS)))
    m_row = [jnp.max(m, axis=0, keepdims=True) for m in m8]
    acc_ref[...] = jnp.zeros(acc_ref.shape, F32)

    def pv_body(c, l8):
        start = chunk_start(c)
        out = []
        for g in range(DSA_KV_HEADS):
            p = jnp.exp2(t_ref[g, pl.ds(start, K_CHUNK), :] - m_row[g])
            vt = vt_ref[0, g * DSA_HEAD_DIM:(g + 1) * DSA_HEAD_DIM, pl.ds(start, K_CHUNK)]
            acc_ref[g] += jnp.dot(vt, p.astype(BF16), preferred_element_type=F32)
            out.append(l8[g] + fold_rows(p, jnp.sum))
        return tuple(out)

    l8 = lax.fori_loop(0, n_chunks, pv_body,
                       tuple(jnp.zeros((8, cols), F32) for _ in range(DSA_KV_HEADS)))
    for g in range(DSA_KV_HEADS):
        o_t = acc_ref[g] / jnp.sum(l8[g], axis=0, keepdims=True)
        for j in range(DSA_GROUP):
            h = g * DSA_GROUP + j
            o_ref[0, :, h * DSA_HEAD_DIM:(h + 1) * DSA_HEAD_DIM] = (
                o_t[:, j * Q_TILE:(j + 1) * Q_TILE].T.astype(BF16))


def _dsa_attn(qt, qit, wit, k, vt, ki):
    B, _, S = qt.shape
    top_k = min(TOPK_MAX, S // 4)
    cols = DSA_GROUP * Q_TILE
    return pl.pallas_call(
        functools.partial(_dsa_attn_kernel, top_k=top_k),
        grid=(B, S // Q_TILE),
        in_specs=[
            pl.BlockSpec((1, DSA_WIDTH, Q_TILE), lambda b, i: (b, 0, i)),
            pl.BlockSpec((1, IDX_HEADS * IDX_DIM, Q_TILE), lambda b, i: (b, 0, i)),
            pl.BlockSpec((1, IDX_HEADS, Q_TILE), lambda b, i: (b, 0, i)),
            pl.BlockSpec((1, DSA_KV_HEADS, S, DSA_HEAD_DIM), lambda b, i: (b, 0, 0, 0)),
            pl.BlockSpec((1, DSA_KV_HEADS * DSA_HEAD_DIM, S), lambda b, i: (b, 0, 0)),
            pl.BlockSpec((1, S, IDX_DIM), lambda b, i: (b, 0, 0)),
        ],
        out_specs=pl.BlockSpec((1, Q_TILE, DSA_WIDTH), lambda b, i: (b, i, 0)),
        out_shape=jax.ShapeDtypeStruct((B, S, DSA_WIDTH), BF16),
        scratch_shapes=[
            pltpu.VMEM((S, Q_TILE), F32),
            pltpu.VMEM((WORD, S // WORD, Q_TILE), jnp.int32),
            pltpu.VMEM((DSA_KV_HEADS, S, cols), F32),
            pltpu.VMEM((DSA_KV_HEADS, DSA_HEAD_DIM, cols), F32),
        ],
        compiler_params=_params(),
        name="dsa_attn",
    )(qt, qit, wit, k, vt, ki)


def _post_kernel(o_ref, z_ref, x_ref, p_ref, w_out_ref, pn_ref, w_gate_ref, w_proj_ref, fn_ref,
                 y_ref, *, final):
    z = z_ref[0].astype(F32)
    gated = o_ref[0].astype(F32) * (z / (1.0 + jnp.exp(-z)))
    x1 = x_ref[0] + jnp.dot(gated.astype(BF16), w_out_ref[...], preferred_element_type=F32)
    pre = jnp.dot(_rms(x1, pn_ref[...]).astype(BF16), w_gate_ref[...], preferred_element_type=F32)
    gate = 1.0 / (1.0 + jnp.exp(-pre))
    emb = jnp.dot(p_ref[0, 0].astype(BF16), w_proj_ref[...], preferred_element_type=F32)
    x2 = x1 + emb * gate
    y_ref[0] = _rms(x2, fn_ref[...]) if final else x2


def _post(o, z, x, p, layer, w_out, pn, w_gate, w_proj, fn, final):
    B, S, _ = x.shape
    T = ROW_TILE
    full = lambda shape: pl.BlockSpec(shape, lambda b, t: (0,) * len(shape))
    tok = lambda d: pl.BlockSpec((1, T, d), lambda b, t: (b, t, 0))
    return pl.pallas_call(
        functools.partial(_post_kernel, final=final),
        grid=(B, S // T),
        in_specs=[
            tok(D_MODEL), tok(D_MODEL), tok(D_MODEL),
            pl.BlockSpec((1, 1, T, PLE_DIM), lambda b, t: (layer, b, t, 0)),
            full((D_MODEL, D_MODEL)), full((1, D_MODEL)), full((D_MODEL, D_MODEL)),
            full((PLE_DIM, D_MODEL)), full((1, D_MODEL)),
        ],
        out_specs=tok(D_MODEL),
        out_shape=jax.ShapeDtypeStruct((B, S, D_MODEL), F32),
        compiler_params=_params(),
        name="post",
    )(o, z, x, p, w_out.astype(BF16), pn.reshape(1, -1), w_gate.astype(BF16), w_proj.astype(BF16),
      fn.reshape(1, -1))


def kernel(x, p, positions, norm_in, mla_w_in, mla_q_norm, mla_kv_norm, mla_w_uq, mla_w_uk, mla_w_uv,
           mla_w_out, dsa_w_in, dsa_w_out, ple_w_proj, ple_norm, ple_w_gate, final_norm):
    cos_t, sin_t, cos_r, sin_r = _rope_tables(positions)
    for i in range(DEPTH):
        j = i // 2
        if i % 2 == 0:
            q_cat, kv_cat, z = _mla_proj(x, cos_r, sin_r, norm_in[i], mla_w_in[j], mla_q_norm[j],
                                         mla_kv_norm[j], mla_w_uq[j], mla_w_uk[j])
            o = _mla_attn(q_cat, kv_cat, mla_w_uv[j])
            w_out = mla_w_out[j]
        else:
            qt, qit, k, ki, vt, wit, z = _dsa_proj(x, cos_t, sin_t, norm_in[i], dsa_w_in[j])
            o = _dsa_attn(qt, qit, wit, k, vt, ki)
            w_out = dsa_w_out[j]
        x = _post(o, z, x, p, i, w_out, ple_norm[i], ple_w_gate[i], ple_w_proj[i], final_norm,
                  final=(i == DEPTH - 1))
    return x
```

```python
import functools

import jax
import jax.numpy as jnp
import numpy as np
from jax import lax
from jax.experimental import pallas as pl
from jax.experimental.pallas import tpu as pltpu

D_MODEL = 1024
DEPTH = 4
PLE_DIM = 256
ROPE_THETA = 10000.0
NORM_EPS = 1e-6

MLA_HEADS = 8
MLA_Q_RANK = 256
MLA_KV_RANK = 256
MLA_NOPE_DIM = 128
MLA_ROPE_DIM = 64
MLA_V_DIM = 128
MLA_WIDTH = MLA_HEADS * MLA_V_DIM
MLA_QK_PAD = MLA_KV_RANK + 128

DSA_HEADS = 8
DSA_KV_HEADS = 2
DSA_GROUP = DSA_HEADS // DSA_KV_HEADS
DSA_HEAD_DIM = 128
DSA_WIDTH = DSA_HEADS * DSA_HEAD_DIM
IDX_HEADS = 8
IDX_DIM = 64
TOPK_MAX = 256
IDX_SCALE = IDX_DIM ** -0.5
assert IDX_SCALE == 0.125

LANES = 128
NEG = -1e30
LOG2E = 1.4426950408889634
VMEM_LIMIT = 56 * 1024 * 1024

ROW_TILE = 512
Q_TILE = 128
MLA_Q_TILE = 256
K_CHUNK = 512

NT_DIMS = (((1,), (1,)), ((), ()))
BF16 = jnp.bfloat16
F32 = jnp.float32


def _rms(x, g):
    return x * lax.rsqrt(jnp.mean(x * x, axis=-1, keepdims=True) + NORM_EPS) * g


def _params(n_parallel=2):
    return pltpu.CompilerParams(
        dimension_semantics=("arbitrary",) * n_parallel,
        vmem_limit_bytes=VMEM_LIMIT)


def _rope_table_kernel(pos_row_ref, pos_col_ref, inv_col_ref, inv_row_ref, sgn_row_ref,
                       cos_t_ref, sin_t_ref, cos_r_ref, sin_r_ref):
    ang_t = inv_col_ref[...] * pos_row_ref[0]
    cos_t_ref[0] = jnp.cos(ang_t)
    sin_t_ref[0] = jnp.sin(ang_t)
    ang_r = pos_col_ref[0] * inv_row_ref[...]
    cos_r_ref[0] = jnp.cos(ang_r)
    sin_r_ref[0] = jnp.sin(ang_r) * sgn_row_ref[...]


def _rope_tables(positions):
    B, S = positions.shape
    T = 512
    posf = positions.astype(F32)
    inv64 = 1.0 / (ROPE_THETA ** (jnp.arange(64, dtype=F32) / 64))
    inv32 = 1.0 / (ROPE_THETA ** (jnp.arange(32, dtype=F32) / 32))
    inv_col = jnp.concatenate([inv64, inv32]).reshape(96, 1)
    inv_row = jnp.tile(inv32, 4).reshape(1, LANES)
    sgn_row = jnp.tile(jnp.concatenate([-jnp.ones(32, F32), jnp.ones(32, F32)]), 2).reshape(1, LANES)
    full = lambda shape: pl.BlockSpec(shape, lambda b, t: (0,) * len(shape))
    return pl.pallas_call(
        _rope_table_kernel,
        grid=(B, S // T),
        in_specs=[
            pl.BlockSpec((1, 1, T), lambda b, t: (b, 0, t)),
            pl.BlockSpec((1, T, 1), lambda b, t: (b, t, 0)),
            full((96, 1)), full((1, LANES)), full((1, LANES)),
        ],
        out_specs=[
            pl.BlockSpec((1, 96, T), lambda b, t: (b, 0, t)),
            pl.BlockSpec((1, 96, T), lambda b, t: (b, 0, t)),
            pl.BlockSpec((1, T, LANES), lambda b, t: (b, t, 0)),
            pl.BlockSpec((1, T, LANES), lambda b, t: (b, t, 0)),
        ],
        out_shape=[
            jax.ShapeDtypeStruct((B, 96, S), F32),
            jax.ShapeDtypeStruct((B, 96, S), F32),
            jax.ShapeDtypeStruct((B, S, LANES), F32),
            jax.ShapeDtypeStruct((B, S, LANES), F32),
        ],
        compiler_params=_params(),
        name="rope_tables",
    )(posf.reshape(B, 1, S), posf.reshape(B, S, 1), inv_col, inv_row, sgn_row)


def _rope_rows(x, cos_r, sin_r):
    lane = lax.broadcasted_iota(jnp.int32, x.shape, 1)
    first = (lane & 63) < 32
    partner = jnp.where(first, pltpu.roll(x, 96, 1), pltpu.roll(x, 32, 1))
    return x * cos_r + partner * sin_r


def _rope_cols(x, cos_t, sin_t):
    half = x.shape[0] // 2
    x1, x2 = x[:half], x[half:]
    return x1 * cos_t - x2 * sin_t, x2 * cos_t + x1 * sin_t


def _mla_proj_kernel(x_ref, cos_r_ref, sin_r_ref, g_ref, w_in_ref, qn_ref, kvn_ref,
                     w_uq_ref, w_uk_ref, q_ref, kv_ref, z_ref):
    hb = _rms(x_ref[0], g_ref[...]).astype(BF16)
    pr = jnp.dot(hb, w_in_ref[...], preferred_element_type=F32)
    cos_r, sin_r = cos_r_ref[0], sin_r_ref[0]
    c_q = _rms(pr[:, :256], qn_ref[...]).astype(BF16)
    c_kv = _rms(pr[:, 256:512], kvn_ref[...])
    z_ref[0] = pr[:, 512:1536].astype(BF16)
    kv_ref[0, :, :256] = c_kv.astype(BF16)
    kv_ref[0, :, 256:] = _rope_rows(pr[:, 1536:1664], cos_r, sin_r).astype(BF16)
    q = jnp.dot(c_q, w_uq_ref[...], preferred_element_type=F32)
    for h in range(MLA_HEADS):
        q_nope = q[:, h * 128:(h + 1) * 128].astype(BF16)
        q_lat = jnp.dot(q_nope, w_uk_ref[h], preferred_element_type=F32)
        q_ref[0, h, :, :256] = q_lat.astype(BF16)
        q_pe = q[:, 1024 + h * 128:1024 + (h + 1) * 128]
        q_ref[0, h, :, 256:] = _rope_rows(q_pe, cos_r, sin_r).astype(BF16)


def _mla_proj(x, cos_r, sin_r, g, w_in, qn, kvn, w_uq, w_uk):
    B, S, _ = x.shape
    T = ROW_TILE
    w_in_r = jnp.concatenate(
        [w_in[:, :512], w_in[:, 576:], w_in[:, 512:576], jnp.zeros((D_MODEL, 64), w_in.dtype)],
        axis=1).astype(BF16)
    w_uq3 = w_uq.reshape(MLA_Q_RANK, MLA_HEADS, MLA_NOPE_DIM + MLA_ROPE_DIM)
    w_pe = jnp.concatenate([w_uq3[:, :, MLA_NOPE_DIM:], jnp.zeros((MLA_Q_RANK, MLA_HEADS, 64), w_uq.dtype)], axis=2)
    w_uq_r = jnp.concatenate(
        [w_uq3[:, :, :MLA_NOPE_DIM].reshape(MLA_Q_RANK, -1), w_pe.reshape(MLA_Q_RANK, -1)],
        axis=1).astype(BF16)
    w_uk_r = jnp.transpose(w_uk, (1, 2, 0)).astype(BF16)
    full = lambda shape: pl.BlockSpec(shape, lambda b, t: (0,) * len(shape))
    return pl.pallas_call(
        _mla_proj_kernel,
        grid=(B, S // T),
        in_specs=[
            pl.BlockSpec((1, T, D_MODEL), lambda b, t: (b, t, 0)),
            pl.BlockSpec((1, T, LANES), lambda b, t: (b, t, 0)),
            pl.BlockSpec((1, T, LANES), lambda b, t: (b, t, 0)),
            full((1, D_MODEL)), full(w_in_r.shape), full((1, 256)), full((1, 256)),
            full(w_uq_r.shape), full(w_uk_r.shape),
        ],
        out_specs=[
            pl.BlockSpec((1, MLA_HEADS, T, MLA_QK_PAD), lambda b, t: (b, 0, t, 0)),
            pl.BlockSpec((1, T, MLA_QK_PAD), lambda b, t: (b, t, 0)),
            pl.BlockSpec((1, T, MLA_WIDTH), lambda b, t: (b, t, 0)),
        ],
        out_shape=[
            jax.ShapeDtypeStruct((B, MLA_HEADS, S, MLA_QK_PAD), BF16),
            jax.ShapeDtypeStruct((B, S, MLA_QK_PAD), BF16),
            jax.ShapeDtypeStruct((B, S, MLA_WIDTH), BF16),
        ],
        compiler_params=_params(),
        name="mla_proj",
    )(x, cos_r, sin_r, g.reshape(1, -1), w_in_r, qn.reshape(1, -1), kvn.reshape(1, -1), w_uq_r, w_uk_r)


def _lane_fold(x, op):
    out = x[:, :LANES]
    for j in range(1, x.shape[1] // LANES):
        out = op(out, x[:, j * LANES:(j + 1) * LANES])
    return out


def _mla_attn_kernel(q_ref, kv_ref, w_uv_ref, o_ref, t_ref, mpart_ref, lpart_ref, acc_ref):
    qb = pl.program_id(1)
    QT = MLA_Q_TILE
    rows = MLA_HEADS * QT
    scale2 = (MLA_NOPE_DIM + MLA_ROPE_DIM) ** -0.5 * LOG2E
    q = q_ref[0].reshape(rows, MLA_QK_PAD)
    n_full = (qb * QT) // K_CHUNK
    mpart_ref[...] = jnp.full(mpart_ref.shape, NEG, F32)

    def chunk_start(c):
        return pl.multiple_of(c * K_CHUNK, K_CHUNK)

    def score_step(c, masked):
        start = chunk_start(c)
        kv = kv_ref[0, pl.ds(start, K_CHUNK), :]
        t = lax.dot_general(q, kv, NT_DIMS, preferred_element_type=F32) * scale2
        if masked:
            kpos = start + lax.broadcasted_iota(jnp.int32, (QT, K_CHUNK), 1)
            qpos = qb * QT + lax.broadcasted_iota(jnp.int32, (QT, K_CHUNK), 0)
            keep = (kpos <= qpos)[None]
            t = jnp.where(keep, t.reshape(MLA_HEADS, QT, K_CHUNK), NEG).reshape(rows, K_CHUNK)
        t_ref[:, pl.ds(start, K_CHUNK)] = t
        mpart_ref[...] = jnp.maximum(mpart_ref[...], _lane_fold(t, jnp.maximum))

    def score_body(c, carry):
        score_step(c, False)
        return carry

    lax.fori_loop(0, n_full, score_body, 0)
    score_step(n_full, True)

    m = jnp.max(mpart_ref[...], axis=-1, keepdims=True)
    mpart_ref[...] = jnp.broadcast_to(m, (rows, LANES))
    lpart_ref[...] = jnp.zeros(lpart_ref.shape, F32)
    acc_ref[...] = jnp.zeros(acc_ref.shape, F32)

    def pv_body(c, carry):
        start = chunk_start(c)
        t = t_ref[:, pl.ds(start, K_CHUNK)]
        m_b = mpart_ref[...]
        p = jnp.concatenate(
            [jnp.exp2(t[:, j * LANES:(j + 1) * LANES] - m_b) for j in range(K_CHUNK // LANES)], axis=1)
        lpart_ref[...] += _lane_fold(p, jnp.add)
        v = kv_ref[0, pl.ds(start, K_CHUNK), :MLA_KV_RANK]
        acc_ref[...] += jnp.dot(p.astype(BF16), v, preferred_element_type=F32)
        return carry

    lax.fori_loop(0, n_full + 1, pv_body, 0)

    o_lat = acc_ref[...] / jnp.sum(lpart_ref[...], axis=-1, keepdims=True)
    for h in range(MLA_HEADS):
        o_h = jnp.dot(o_lat[h * QT:(h + 1) * QT].astype(BF16), w_uv_ref[h],
                      preferred_element_type=F32)
        o_ref[0, :, h * MLA_V_DIM:(h + 1) * MLA_V_DIM] = o_h.astype(BF16)


def _mla_attn(q_cat, kv_cat, w_uv):
    B, H, S, _ = q_cat.shape
    w_uv_r = jnp.transpose(w_uv, (1, 0, 2)).astype(BF16)
    QT = MLA_Q_TILE
    rows = H * QT
    return pl.pallas_call(
        _mla_attn_kernel,
        grid=(B, S // QT),
        in_specs=[
            pl.BlockSpec((1, H, QT, MLA_QK_PAD), lambda b, i: (b, 0, i, 0)),
            pl.BlockSpec((1, S, MLA_QK_PAD), lambda b, i: (b, 0, 0)),
            pl.BlockSpec(w_uv_r.shape, lambda b, i: (0, 0, 0)),
        ],
        out_specs=pl.BlockSpec((1, QT, MLA_WIDTH), lambda b, i: (b, i, 0)),
        out_shape=jax.ShapeDtypeStruct((B, S, MLA_WIDTH), BF16),
        scratch_shapes=[
            pltpu.VMEM((rows, S), F32),
            pltpu.VMEM((rows, LANES), F32),
            pltpu.VMEM((rows, LANES), F32),
            pltpu.VMEM((rows, MLA_KV_RANK), F32),
        ],
        compiler_params=_params(),
        name="mla_attn",
    )(q_cat, kv_cat, w_uv_r)


DSA_ROW_Q = 0
DSA_ROW_QI = DSA_ROW_Q + DSA_WIDTH
DSA_ROW_K = DSA_ROW_QI + IDX_HEADS * IDX_DIM
DSA_ROW_KI = DSA_ROW_K + DSA_KV_HEADS * DSA_HEAD_DIM
DSA_ROW_V = DSA_ROW_KI + 128
DSA_ROW_WI = DSA_ROW_V + DSA_KV_HEADS * DSA_HEAD_DIM
DSA_ROWS = DSA_ROW_WI + 16


def _dsa_proj_kernel(x_ref, cos_t_ref, sin_t_ref, g_ref, w_f_ref, w_z_ref,
                     qt_ref, qit_ref, k_ref, ki_ref, vt_ref, wit_ref, z_ref):
    hb = _rms(x_ref[0], g_ref[...]).astype(BF16)
    z_ref[0] = jnp.dot(hb, w_z_ref[...], preferred_element_type=F32).astype(BF16)
    pt = lax.dot_general(w_f_ref[...], hb, NT_DIMS, preferred_element_type=F32)
    c64, s64 = cos_t_ref[0, :64], sin_t_ref[0, :64]
    c32, s32 = cos_t_ref[0, 64:96], sin_t_ref[0, 64:96]
    for h in range(DSA_HEADS):
        r = DSA_ROW_Q + h * 128
        o1, o2 = _rope_cols(pt[r:r + 128], c64, s64)
        qt_ref[0, r:r + 64] = o1.astype(BF16)
        qt_ref[0, r + 64:r + 128] = o2.astype(BF16)
    for h in range(IDX_HEADS):
        r = DSA_ROW_QI + h * 64
        o1, o2 = _rope_cols(pt[r:r + 64], c32, s32)
        qit_ref[0, h * 64:h * 64 + 32] = (o1 * IDX_SCALE).astype(BF16)
        qit_ref[0, h * 64 + 32:h * 64 + 64] = (o2 * IDX_SCALE).astype(BF16)
    for g in range(DSA_KV_HEADS):
        r = DSA_ROW_K + g * 128
        o1, o2 = _rope_cols(pt[r:r + 128], c64, s64)
        k_ref[0, g] = jnp.concatenate([o1, o2], axis=0).T.astype(BF16)
    o1, o2 = _rope_cols(pt[DSA_ROW_KI:DSA_ROW_KI + 64], c32, s32)
    ki_pad = jnp.concatenate([o1, o2, pt[DSA_ROW_KI + 64:DSA_ROW_KI + 128]], axis=0)
    ki_ref[0] = ki_pad.T[:, :IDX_DIM].astype(BF16)
    vt_ref[0] = pt[DSA_ROW_V:DSA_ROW_V + 256].astype(BF16)
    wit_ref[0] = pt[DSA_ROW_WI:DSA_ROW_WI + 8] * (IDX_HEADS ** -0.5)


def _dsa_proj(x, cos_t, sin_t, g, w_in):
    B, S, _ = x.shape
    T = ROW_TILE
    c = 0
    q_w = w_in[:, c:c + 1024]; c += 1024
    k_w = w_in[:, c:c + 256]; c += 256
    v_w = w_in[:, c:c + 256]; c += 256
    qi_w = w_in[:, c:c + 512]; c += 512
    ki_w = w_in[:, c:c + 64]; c += 64
    wi_w = w_in[:, c:c + 8]; c += 8
    z_w = w_in[:, c:c + 1024]
    zeros = lambda n: jnp.zeros((D_MODEL, n), w_in.dtype)
    w_f = jnp.concatenate([q_w, qi_w, k_w, ki_w, zeros(64), v_w, wi_w, zeros(8)], axis=1).T.astype(BF16)
    assert w_f.shape == (DSA_ROWS, D_MODEL)
    full = lambda shape: pl.BlockSpec(shape, lambda b, t: (0,) * len(shape))
    return pl.pallas_call(
        _dsa_proj_kernel,
        grid=(B, S // T),
        in_specs=[
            pl.BlockSpec((1, T, D_MODEL), lambda b, t: (b, t, 0)),
            pl.BlockSpec((1, 96, T), lambda b, t: (b, 0, t)),
            pl.BlockSpec((1, 96, T), lambda b, t: (b, 0, t)),
            full((1, D_MODEL)), full(w_f.shape), full((D_MODEL, DSA_WIDTH)),
        ],
        out_specs=[
            pl.BlockSpec((1, DSA_WIDTH, T), lambda b, t: (b, 0, t)),
            pl.BlockSpec((1, IDX_HEADS * IDX_DIM, T), lambda b, t: (b, 0, t)),
            pl.BlockSpec((1, DSA_KV_HEADS, T, DSA_HEAD_DIM), lambda b, t: (b, 0, t, 0)),
            pl.BlockSpec((1, T, IDX_DIM), lambda b, t: (b, t, 0)),
            pl.BlockSpec((1, DSA_KV_HEADS * DSA_HEAD_DIM, T), lambda b, t: (b, 0, t)),
            pl.BlockSpec((1, IDX_HEADS, T), lambda b, t: (b, 0, t)),
            pl.BlockSpec((1, T, DSA_WIDTH), lambda b, t: (b, t, 0)),
        ],
        out_shape=[
            jax.ShapeDtypeStruct((B, DSA_WIDTH, S), BF16),
            jax.ShapeDtypeStruct((B, IDX_HEADS * IDX_DIM, S), BF16),
            jax.ShapeDtypeStruct((B, DSA_KV_HEADS, S, DSA_HEAD_DIM), BF16),
            jax.ShapeDtypeStruct((B, S, IDX_DIM), BF16),
            jax.ShapeDtypeStruct((B, DSA_KV_HEADS * DSA_HEAD_DIM, S), BF16),
            jax.ShapeDtypeStruct((B, IDX_HEADS, S), F32),
            jax.ShapeDtypeStruct((B, S, DSA_WIDTH), BF16),
        ],
        compiler_params=_params(),
        name="dsa_proj",
    )(x, cos_t, sin_t, g.reshape(1, -1), w_f, z_w.astype(BF16))


INT_MIN = np.int32(-2 ** 31)
WORD = 32
PLANE_ROWS = K_CHUNK // WORD


def _float_to_ukey(x):
    bits = lax.bitcast_convert_type(x, jnp.int32)
    return bits ^ ((bits >> 31) | INT_MIN)


def _ukey_to_float(u):
    bits = jnp.where(u < 0, u ^ INT_MIN, ~u)
    return lax.bitcast_convert_type(bits, F32)


def _bit_transpose32(words):
    a = list(words)
    j, m = 16, 0x0000FFFF
    while j:
        mask = np.uint32(m).astype(np.int32)
        k = 0
        while k < 32:
            t = (a[k] ^ lax.shift_right_logical(a[k + j], jnp.int32(j))) & mask
            a[k] = a[k] ^ t
            a[k + j] = a[k + j] ^ lax.shift_left(t, jnp.int32(j))
            k = (k + j + 1) & ~j
        j >>= 1
        m = (m ^ (m << j)) & 0xFFFFFFFF
    return a


def _dsa_attn_kernel(qt_ref, qit_ref, wit_ref, k_ref, vt_ref, ki_ref, o_ref,
                     sc_ref, planes_ref, t_ref, acc_ref, *, top_k):
    qb = pl.program_id(1)
    n_chunks = qb // (K_CHUNK // Q_TILE) + 1
    qpos = qb * Q_TILE + lax.broadcasted_iota(jnp.int32, (K_CHUNK, Q_TILE), 1)

    def chunk_start(c):
        return pl.multiple_of(c * K_CHUNK, K_CHUNK)

    qi_all = jnp.concatenate(
        [qit_ref[0, h * IDX_DIM:(h + 1) * IDX_DIM, :] for h in range(IDX_HEADS)], axis=1)
    w_rows = [wit_ref[0, h:h + 1, :] for h in range(IDX_HEADS)]

    def score_body(c, carry):
        start = chunk_start(c)
        ki = ki_ref[0, pl.ds(start, K_CHUNK), :]
        lg = jnp.dot(ki, qi_all, preferred_element_type=F32)
        sc = jnp.zeros((K_CHUNK, Q_TILE), F32)
        for h in range(IDX_HEADS):
            sc = sc + jnp.maximum(lg[:, h * Q_TILE:(h + 1) * Q_TILE], 0.0) * w_rows[h]
        kpos = start + lax.broadcasted_iota(jnp.int32, (K_CHUNK, Q_TILE), 0)
        sc = jnp.where(kpos <= qpos, sc, -jnp.inf)
        sc_ref[pl.ds(start, K_CHUNK), :] = sc
        u = _float_to_ukey(sc)
        for half in range(K_CHUNK // (8 * WORD)):
            base = half * 8 * WORD
            planes = _bit_transpose32([u[base + 8 * v:base + 8 * v + 8] for v in range(WORD)])
            row = pl.multiple_of(c * PLANE_ROWS + half * 8, 8)
            for i in range(WORD):
                planes_ref[i, pl.ds(row, 8), :] = planes[i]
        return carry

    lax.fori_loop(0, n_chunks, score_body, 0)

    n_rows = planes_ref.shape[1]

    def lane_count(words):
        pc = lax.population_count(words).astype(F32)
        a = jnp.sum(pc.reshape(4, n_rows // 4, Q_TILE), axis=0)
        a = jnp.sum(a.reshape(n_rows // 32, 8, Q_TILE), axis=0)
        return jnp.sum(a, axis=0, keepdims=True)

    def search(_):
        def zero_fill(c, carry):
            row = pl.multiple_of(c * PLANE_ROWS, PLANE_ROWS)
            planes_ref[:, pl.ds(row, PLANE_ROWS), :] = jnp.zeros((WORD, PLANE_ROWS, Q_TILE), jnp.int32)
            return carry
        lax.fori_loop(n_chunks, n_rows // PLANE_ROWS, zero_fill, 0)

        row_id = lax.broadcasted_iota(jnp.int32, (n_rows, Q_TILE), 0)
        alive0 = jnp.where(row_id < n_chunks * PLANE_ROWS, jnp.int32(-1), jnp.int32(0))

        def bit_body(i, carry):
            alive, rem, t = carry
            ones = alive & planes_ref[i]
            n = lane_count(ones)
            take = n >= rem
            alive = jnp.where(take, ones, alive ^ ones)
            rem = jnp.where(take, rem, rem - n)
            t = jnp.where(take, t | lax.shift_left(jnp.int32(1), 31 - i), t)
            return alive, rem, t

        alive, rem, t = lax.fori_loop(
            0, 32, bit_body,
            (alive0, jnp.full((1, Q_TILE), float(top_k), F32), jnp.zeros((1, Q_TILE), jnp.int32)))
        return _ukey_to_float(t), rem, lane_count(alive) - rem

    def take_all(_):
        return (jnp.full((1, Q_TILE), jnp.finfo(F32).min, F32), jnp.zeros((1, Q_TILE), F32),
                jnp.zeros((1, Q_TILE), F32))

    thr, need, surplus = lax.cond((qb + 1) * Q_TILE > top_k, search, take_all, 0)
    has_tie = jnp.max(surplus) > 0.0

    @pl.when(jnp.logical_not(has_tie))
    def _():
        def body(c, carry):
            sl = pl.ds(chunk_start(c), K_CHUNK)
            sc_ref[sl, :] = jnp.where(sc_ref[sl, :] >= thr, 0.0, NEG)
            return carry
        lax.fori_loop(0, n_chunks, body, 0)

    @pl.when(has_tie)
    def _():
        r_i = lax.broadcasted_iota(jnp.int32, (K_CHUNK, K_CHUNK), 0)
        c_i = lax.broadcasted_iota(jnp.int32, (K_CHUNK, K_CHUNK), 1)
        before = jnp.where(c_i < r_i, 1.0, 0.0).astype(BF16)

        def body(c, seen):
            sl = pl.ds(chunk_start(c), K_CHUNK)
            blk = sc_ref[sl, :]
            eq = jnp.where(blk == thr, 1.0, 0.0)
            rank = seen + jnp.dot(before, eq.astype(BF16), preferred_element_type=F32)
            keep = jnp.where(blk > thr, 1.0, jnp.where(rank < need, eq, 0.0))
            sc_ref[sl, :] = jnp.where(keep > 0.0, 0.0, NEG)
            return seen + jnp.sum(eq, axis=0, keepdims=True)
        lax.fori_loop(0, n_chunks, body, jnp.zeros((1, Q_TILE), F32))

    scale2 = DSA_HEAD_DIM ** -0.5 * LOG2E
    cols = DSA_GROUP * Q_TILE
    q_grp = [
        jnp.concatenate(
            [qt_ref[0, (g * DSA_GROUP + j) * DSA_HEAD_DIM:(g * DSA_GROUP + j + 1) * DSA_HEAD_DIM, :]
             for j in range(DSA_GROUP)], axis=1)
        for g in range(DSA_KV_HEADS)]

    def fold_rows(x, op):
        a = op(x.reshape(8, K_CHUNK // 8, x.shape[1]), axis=0)
        return op(a.reshape(8, 8, x.shape[1]), axis=0)

    def score_body(c, m8):
        start = chunk_start(c)
        bias = sc_ref[pl.ds(start, K_CHUNK), :]
        out = []
        for g in range(DSA_KV_HEADS):
            k = k_ref[0, g, pl.ds(start, K_CHUNK), :]
            s = jnp.dot(k, q_grp[g], preferred_element_type=F32)
            col_max = []
            for j in range(DSA_GROUP):
                t = s[:, j * Q_TILE:(j + 1) * Q_TILE] * scale2 + bias
                t_ref[g, pl.ds(start, K_CHUNK), j * Q_TILE:(j + 1) * Q_TILE] = t
                col_max.append(fold_rows(t, jnp.max))
            out.append(jnp.maximum(m8[g], jnp.concatenate(col_max, axis=1)))
        return tuple(out)

    m8 = lax.fori_loop(0, n_chunks, score_body,
                       tuple(jnp.full((8, cols), NEG, F32) for _ in range(DSA_KV_HEADS)))
    m_row = [jnp.max(m, axis=0, keepdims=True) for m in m8]
    acc_ref[...] = jnp.zeros(acc_ref.shape, F32)

    def pv_body(c, l8):
        start = chunk_start(c)
        out = []
        for g in range(DSA_KV_HEADS):
            p = jnp.exp2(t_ref[g, pl.ds(start, K_CHUNK), :] - m_row[g])
            vt = vt_ref[0, g * DSA_HEAD_DIM:(g + 1) * DSA_HEAD_DIM, pl.ds(start, K_CHUNK)]
            acc_ref[g] += jnp.dot(vt, p.astype(BF16), preferred_element_type=F32)
            out.append(l8[g] + fold_rows(p, jnp.sum))
        return tuple(out)

    l8 = lax.fori_loop(0, n_chunks, pv_body,
                       tuple(jnp.zeros((8, cols), F32) for _ in range(DSA_KV_HEADS)))
    for g in range(DSA_KV_HEADS):
        o_t = acc_ref[g] / jnp.sum(l8[g], axis=0, keepdims=True)
        for j in range(DSA_GROUP):
            h = g * DSA_GROUP + j
            o_ref[0, :, h * DSA_HEAD_DIM:(h + 1) * DSA_HEAD_DIM] = (
                o_t[:, j * Q_TILE:(j + 1) * Q_TILE].T.astype(BF16))


def _dsa_attn(qt, qit, wit, k, vt, ki):
    B, _, S = qt.shape
    top_k = min(TOPK_MAX, S // 4)
    cols = DSA_GROUP * Q_TILE
    return pl.pallas_call(
        functools.partial(_dsa_attn_kernel, top_k=top_k),
        grid=(B, S // Q_TILE),
        in_specs=[
            pl.BlockSpec((1, DSA_WIDTH, Q_TILE), lambda b, i: (b, 0, i)),
            pl.BlockSpec((1, IDX_HEADS * IDX_DIM, Q_TILE), lambda b, i: (b, 0, i)),
            pl.BlockSpec((1, IDX_HEADS, Q_TILE), lambda b, i: (b, 0, i)),
            pl.BlockSpec((1, DSA_KV_HEADS, S, DSA_HEAD_DIM), lambda b, i: (b, 0, 0, 0)),
            pl.BlockSpec((1, DSA_KV_HEADS * DSA_HEAD_DIM, S), lambda b, i: (b, 0, 0)),
            pl.BlockSpec((1, S, IDX_DIM), lambda b, i: (b, 0, 0)),
        ],
        out_specs=pl.BlockSpec((1, Q_TILE, DSA_WIDTH), lambda b, i: (b, i, 0)),
        out_shape=jax.ShapeDtypeStruct((B, S, DSA_WIDTH), BF16),
        scratch_shapes=[
            pltpu.VMEM((S, Q_TILE), F32),
            pltpu.VMEM((WORD, S // WORD, Q_TILE), jnp.int32),
            pltpu.VMEM((DSA_KV_HEADS, S, cols), F32),
            pltpu.VMEM((DSA_KV_HEADS, DSA_HEAD_DIM, cols), F32),
        ],
        compiler_params=_params(),
        name="dsa_attn",
    )(qt, qit, wit, k, vt, ki)


def _post_kernel(o_ref, z_ref, x_ref, p_ref, w_out_ref, pn_ref, w_gate_ref, w_proj_ref, fn_ref,
                 y_ref, *, final):
    z = z_ref[0].astype(F32)
    gated = o_ref[0].astype(F32) * (z / (1.0 + jnp.exp(-z)))
    x1 = x_ref[0] + jnp.dot(gated.astype(BF16), w_out_ref[...], preferred_element_type=F32)
    pre = jnp.dot(_rms(x1, pn_ref[...]).astype(BF16), w_gate_ref[...], preferred_element_type=F32)
    gate = 1.0 / (1.0 + jnp.exp(-pre))
    emb = jnp.dot(p_ref[0, 0].astype(BF16), w_proj_ref[...], preferred_element_type=F32)
    x2 = x1 + emb * gate
    y_ref[0] = _rms(x2, fn_ref[...]) if final else x2


def _post(o, z, x, p, layer, w_out, pn, w_gate, w_proj, fn, final):
    B, S, _ = x.shape
    T = ROW_TILE
    full = lambda shape: pl.BlockSpec(shape, lambda b, t: (0,) * len(shape))
    tok = lambda d: pl.BlockSpec((1, T, d), lambda b, t: (b, t, 0))
    return pl.pallas_call(
        functools.partial(_post_kernel, final=final),
        grid=(B, S // T),
        in_specs=[
            tok(D_MODEL), tok(D_MODEL), tok(D_MODEL),
            pl.BlockSpec((1, 1, T, PLE_DIM), lambda b, t: (layer, b, t, 0)),
            full((D_MODEL, D_MODEL)), full((1, D_MODEL)), full((D_MODEL, D_MODEL)),
            full((PLE_DIM, D_MODEL)), full((1, D_MODEL)),
        ],
        out_specs=tok(D_MODEL),
        out_shape=jax.ShapeDtypeStruct((B, S, D_MODEL), F32),
        compiler_params=_params(),
        name="post",
    )(o, z, x, p, w_out.astype(BF16), pn.reshape(1, -1), w_gate.astype(BF16), w_proj.astype(BF16),
      fn.reshape(1, -1))


def kernel(x, p, positions, norm_in, mla_w_in, mla_q_norm, mla_kv_norm, mla_w_uq, mla_w_uk, mla_w_uv,
           mla_w_out, dsa_w_in, dsa_w_out, ple_w_proj, ple_norm, ple_w_gate, final_norm):
    cos_t, sin_t, cos_r, sin_r = _rope_tables(positions)
    for i in range(DEPTH):
        j = i // 2
        if i % 2 == 0:
            q_cat, kv_cat, z = _mla_proj(x, cos_r, sin_r, norm_in[i], mla_w_in[j], mla_q_norm[j],
                                         mla_kv_norm[j], mla_w_uq[j], mla_w_uk[j])
            o = _mla_attn(q_cat, kv_cat, mla_w_uv[j])
            w_out = mla_w_out[j]
        else:
            qt, qit, k, ki, vt, wit, z = _dsa_proj(x, cos_t, sin_t, norm_in[i], dsa_w_in[j])
            o = _dsa_attn(qt, qit, wit, k, vt, ki)
            w_out = dsa_w_out[j]
        x = _post(o, z, x, p, i, w_out, ple_norm[i], ple_w_gate[i], ple_w_proj[i], final_norm,
                  final=(i == DEPTH - 1))
    return x
```

```python
import functools

import jax
import jax.numpy as jnp
import numpy as np
from jax import lax
from jax.experimental import pallas as pl
from jax.experimental.pallas import tpu as pltpu

D_MODEL = 1024
DEPTH = 4
PLE_DIM = 256
ROPE_THETA = 10000.0
NORM_EPS = 1e-6

MLA_HEADS = 8
MLA_Q_RANK = 256
MLA_KV_RANK = 256
MLA_NOPE_DIM = 128
MLA_ROPE_DIM = 64
MLA_V_DIM = 128
MLA_WIDTH = MLA_HEADS * MLA_V_DIM
MLA_QK_PAD = MLA_KV_RANK + 128

DSA_HEADS = 8
DSA_KV_HEADS = 2
DSA_GROUP = DSA_HEADS // DSA_KV_HEADS
DSA_HEAD_DIM = 128
DSA_WIDTH = DSA_HEADS * DSA_HEAD_DIM
IDX_HEADS = 8
IDX_DIM = 64
TOPK_MAX = 256
IDX_SCALE = IDX_DIM ** -0.5
assert IDX_SCALE == 0.125

LANES = 128
NEG = -1e30
LOG2E = 1.4426950408889634
VMEM_LIMIT = 56 * 1024 * 1024

ROW_TILE = 512
Q_TILE = 256
MLA_Q_TILE = 256
K_CHUNK = 512

NT_DIMS = (((1,), (1,)), ((), ()))
BF16 = jnp.bfloat16
F32 = jnp.float32


def _rms(x, g):
    return x * lax.rsqrt(jnp.mean(x * x, axis=-1, keepdims=True) + NORM_EPS) * g


def _params(n_parallel=2):
    return pltpu.CompilerParams(
        dimension_semantics=("arbitrary",) * n_parallel,
        vmem_limit_bytes=VMEM_LIMIT)


def _rope_table_kernel(pos_row_ref, pos_col_ref, inv_col_ref, inv_row_ref, sgn_row_ref,
                       cos_t_ref, sin_t_ref, cos_r_ref, sin_r_ref):
    ang_t = inv_col_ref[...] * pos_row_ref[0]
    cos_t_ref[0] = jnp.cos(ang_t)
    sin_t_ref[0] = jnp.sin(ang_t)
    ang_r = pos_col_ref[0] * inv_row_ref[...]
    cos_r_ref[0] = jnp.cos(ang_r)
    sin_r_ref[0] = jnp.sin(ang_r) * sgn_row_ref[...]


def _rope_tables(positions):
    B, S = positions.shape
    T = 512
    posf = positions.astype(F32)
    inv64 = 1.0 / (ROPE_THETA ** (jnp.arange(64, dtype=F32) / 64))
    inv32 = 1.0 / (ROPE_THETA ** (jnp.arange(32, dtype=F32) / 32))
    inv_col = jnp.concatenate([inv64, inv32]).reshape(96, 1)
    inv_row = jnp.tile(inv32, 4).reshape(1, LANES)
    sgn_row = jnp.tile(jnp.concatenate([-jnp.ones(32, F32), jnp.ones(32, F32)]), 2).reshape(1, LANES)
    full = lambda shape: pl.BlockSpec(shape, lambda b, t: (0,) * len(shape))
    return pl.pallas_call(
        _rope_table_kernel,
        grid=(B, S // T),
        in_specs=[
            pl.BlockSpec((1, 1, T), lambda b, t: (b, 0, t)),
            pl.BlockSpec((1, T, 1), lambda b, t: (b, t, 0)),
            full((96, 1)), full((1, LANES)), full((1, LANES)),
        ],
        out_specs=[
            pl.BlockSpec((1, 96, T), lambda b, t: (b, 0, t)),
            pl.BlockSpec((1, 96, T), lambda b, t: (b, 0, t)),
            pl.BlockSpec((1, T, LANES), lambda b, t: (b, t, 0)),
            pl.BlockSpec((1, T, LANES), lambda b, t: (b, t, 0)),
        ],
        out_shape=[
            jax.ShapeDtypeStruct((B, 96, S), F32),
            jax.ShapeDtypeStruct((B, 96, S), F32),
            jax.ShapeDtypeStruct((B, S, LANES), F32),
            jax.ShapeDtypeStruct((B, S, LANES), F32),
        ],
        compiler_params=_params(),
        name="rope_tables",
    )(posf.reshape(B, 1, S), posf.reshape(B, S, 1), inv_col, inv_row, sgn_row)


def _rope_rows(x, cos_r, sin_r):
    lane = lax.broadcasted_iota(jnp.int32, x.shape, 1)
    first = (lane & 63) < 32
    partner = jnp.where(first, pltpu.roll(x, 96, 1), pltpu.roll(x, 32, 1))
    return x * cos_r + partner * sin_r


def _rope_cols(x, cos_t, sin_t):
    half = x.shape[0] // 2
    x1, x2 = x[:half], x[half:]
    return x1 * cos_t - x2 * sin_t, x2 * cos_t + x1 * sin_t


def _mla_proj_kernel(x_ref, cos_r_ref, sin_r_ref, g_ref, w_in_ref, qn_ref, kvn_ref,
                     w_uq_ref, w_uk_ref, q_ref, kv_ref, z_ref):
    hb = _rms(x_ref[0], g_ref[...]).astype(BF16)
    pr = jnp.dot(hb, w_in_ref[...], preferred_element_type=F32)
    cos_r, sin_r = cos_r_ref[0], sin_r_ref[0]
    c_q = _rms(pr[:, :256], qn_ref[...]).astype(BF16)
    c_kv = _rms(pr[:, 256:512], kvn_ref[...])
    z_ref[0] = pr[:, 512:1536].astype(BF16)
    kv_ref[0, :, :256] = c_kv.astype(BF16)
    kv_ref[0, :, 256:] = _rope_rows(pr[:, 1536:1664], cos_r, sin_r).astype(BF16)
    q = jnp.dot(c_q, w_uq_ref[...], preferred_element_type=F32)
    for h in range(MLA_HEADS):
        q_nope = q[:, h * 128:(h + 1) * 128].astype(BF16)
        q_lat = jnp.dot(q_nope, w_uk_ref[h], preferred_element_type=F32)
        q_ref[0, h, :, :256] = q_lat.astype(BF16)
        q_pe = q[:, 1024 + h * 128:1024 + (h + 1) * 128]
        q_ref[0, h, :, 256:] = _rope_rows(q_pe, cos_r, sin_r).astype(BF16)


def _mla_proj(x, cos_r, sin_r, g, w_in, qn, kvn, w_uq, w_uk):
    B, S, _ = x.shape
    T = ROW_TILE
    w_in_r = jnp.concatenate(
        [w_in[:, :512], w_in[:, 576:], w_in[:, 512:576], jnp.zeros((D_MODEL, 64), w_in.dtype)],
        axis=1).astype(BF16)
    w_uq3 = w_uq.reshape(MLA_Q_RANK, MLA_HEADS, MLA_NOPE_DIM + MLA_ROPE_DIM)
    w_pe = jnp.concatenate([w_uq3[:, :, MLA_NOPE_DIM:], jnp.zeros((MLA_Q_RANK, MLA_HEADS, 64), w_uq.dtype)], axis=2)
    w_uq_r = jnp.concatenate(
        [w_uq3[:, :, :MLA_NOPE_DIM].reshape(MLA_Q_RANK, -1), w_pe.reshape(MLA_Q_RANK, -1)],
        axis=1).astype(BF16)
    w_uk_r = jnp.transpose(w_uk, (1, 2, 0)).astype(BF16)
    full = lambda shape: pl.BlockSpec(shape, lambda b, t: (0,) * len(shape))
    return pl.pallas_call(
        _mla_proj_kernel,
        grid=(B, S // T),
        in_specs=[
            pl.BlockSpec((1, T, D_MODEL), lambda b, t: (b, t, 0)),
            pl.BlockSpec((1, T, LANES), lambda b, t: (b, t, 0)),
            pl.BlockSpec((1, T, LANES), lambda b, t: (b, t, 0)),
            full((1, D_MODEL)), full(w_in_r.shape), full((1, 256)), full((1, 256)),
            full(w_uq_r.shape), full(w_uk_r.shape),
        ],
        out_specs=[
            pl.BlockSpec((1, MLA_HEADS, T, MLA_QK_PAD), lambda b, t: (b, 0, t, 0)),
            pl.BlockSpec((1, T, MLA_QK_PAD), lambda b, t: (b, t, 0)),
            pl.BlockSpec((1, T, MLA_WIDTH), lambda b, t: (b, t, 0)),
        ],
        out_shape=[
            jax.ShapeDtypeStruct((B, MLA_HEADS, S, MLA_QK_PAD), BF16),
            jax.ShapeDtypeStruct((B, S, MLA_QK_PAD), BF16),
            jax.ShapeDtypeStruct((B, S, MLA_WIDTH), BF16),
        ],
        compiler_params=_params(),
        name="mla_proj",
    )(x, cos_r, sin_r, g.reshape(1, -1), w_in_r, qn.reshape(1, -1), kvn.reshape(1, -1), w_uq_r, w_uk_r)


def _lane_fold(x, op):
    out = x[:, :LANES]
    for j in range(1, x.shape[1] // LANES):
        out = op(out, x[:, j * LANES:(j + 1) * LANES])
    return out


def _mla_attn_kernel(q_ref, kv_ref, w_uv_ref, o_ref, t_ref, mpart_ref, lpart_ref, acc_ref):
    qb = pl.program_id(1)
    QT = MLA_Q_TILE
    rows = MLA_HEADS * QT
    scale2 = (MLA_NOPE_DIM + MLA_ROPE_DIM) ** -0.5 * LOG2E
    q = q_ref[0].reshape(rows, MLA_QK_PAD)
    n_full = (qb * QT) // K_CHUNK
    mpart_ref[...] = jnp.full(mpart_ref.shape, NEG, F32)

    def chunk_start(c):
        return pl.multiple_of(c * K_CHUNK, K_CHUNK)

    def score_step(c, masked):
        start = chunk_start(c)
        kv = kv_ref[0, pl.ds(start, K_CHUNK), :]
        t = lax.dot_general(q, kv, NT_DIMS, preferred_element_type=F32) * scale2
        if masked:
            kpos = start + lax.broadcasted_iota(jnp.int32, (QT, K_CHUNK), 1)
            qpos = qb * QT + lax.broadcasted_iota(jnp.int32, (QT, K_CHUNK), 0)
            keep = (kpos <= qpos)[None]
            t = jnp.where(keep, t.reshape(MLA_HEADS, QT, K_CHUNK), NEG).reshape(rows, K_CHUNK)
        t_ref[:, pl.ds(start, K_CHUNK)] = t
        mpart_ref[...] = jnp.maximum(mpart_ref[...], _lane_fold(t, jnp.maximum))

    def score_body(c, carry):
        score_step(c, False)
        return carry

    lax.fori_loop(0, n_full, score_body, 0)
    score_step(n_full, True)

    m = jnp.max(mpart_ref[...], axis=-1, keepdims=True)
    mpart_ref[...] = jnp.broadcast_to(m, (rows, LANES))
    lpart_ref[...] = jnp.zeros(lpart_ref.shape, F32)
    acc_ref[...] = jnp.zeros(acc_ref.shape, F32)

    def pv_body(c, carry):
        start = chunk_start(c)
        t = t_ref[:, pl.ds(start, K_CHUNK)]
        m_b = mpart_ref[...]
        p = jnp.concatenate(
            [jnp.exp2(t[:, j * LANES:(j + 1) * LANES] - m_b) for j in range(K_CHUNK // LANES)], axis=1)
        lpart_ref[...] += _lane_fold(p, jnp.add)
        v = kv_ref[0, pl.ds(start, K_CHUNK), :MLA_KV_RANK]
        acc_ref[...] += jnp.dot(p.astype(BF16), v, preferred_element_type=F32)
        return carry

    lax.fori_loop(0, n_full + 1, pv_body, 0)

    o_lat = acc_ref[...] / jnp.sum(lpart_ref[...], axis=-1, keepdims=True)
    for h in range(MLA_HEADS):
        o_h = jnp.dot(o_lat[h * QT:(h + 1) * QT].astype(BF16), w_uv_ref[h],
                      preferred_element_type=F32)
        o_ref[0, :, h * MLA_V_DIM:(h + 1) * MLA_V_DIM] = o_h.astype(BF16)


def _mla_attn(q_cat, kv_cat, w_uv):
    B, H, S, _ = q_cat.shape
    w_uv_r = jnp.transpose(w_uv, (1, 0, 2)).astype(BF16)
    QT = MLA_Q_TILE
    rows = H * QT
    return pl.pallas_call(
        _mla_attn_kernel,
        grid=(B, S // QT),
        in_specs=[
            pl.BlockSpec((1, H, QT, MLA_QK_PAD), lambda b, i: (b, 0, i, 0)),
            pl.BlockSpec((1, S, MLA_QK_PAD), lambda b, i: (b, 0, 0)),
            pl.BlockSpec(w_uv_r.shape, lambda b, i: (0, 0, 0)),
        ],
        out_specs=pl.BlockSpec((1, QT, MLA_WIDTH), lambda b, i: (b, i, 0)),
        out_shape=jax.ShapeDtypeStruct((B, S, MLA_WIDTH), BF16),
        scratch_shapes=[
            pltpu.VMEM((rows, S), F32),
            pltpu.VMEM((rows, LANES), F32),
            pltpu.VMEM((rows, LANES), F32),
            pltpu.VMEM((rows, MLA_KV_RANK), F32),
        ],
        compiler_params=_params(),
        name="mla_attn",
    )(q_cat, kv_cat, w_uv_r)


DSA_ROW_Q = 0
DSA_ROW_QI = DSA_ROW_Q + DSA_WIDTH
DSA_ROW_K = DSA_ROW_QI + IDX_HEADS * IDX_DIM
DSA_ROW_KI = DSA_ROW_K + DSA_KV_HEADS * DSA_HEAD_DIM
DSA_ROW_V = DSA_ROW_KI + 128
DSA_ROW_WI = DSA_ROW_V + DSA_KV_HEADS * DSA_HEAD_DIM
DSA_ROWS = DSA_ROW_WI + 16


def _dsa_proj_kernel(x_ref, cos_t_ref, sin_t_ref, g_ref, w_f_ref, w_z_ref,
                     qt_ref, qit_ref, k_ref, ki_ref, vt_ref, wit_ref, z_ref):
    hb = _rms(x_ref[0], g_ref[...]).astype(BF16)
    z_ref[0] = jnp.dot(hb, w_z_ref[...], preferred_element_type=F32).astype(BF16)
    pt = lax.dot_general(w_f_ref[...], hb, NT_DIMS, preferred_element_type=F32)
    c64, s64 = cos_t_ref[0, :64], sin_t_ref[0, :64]
    c32, s32 = cos_t_ref[0, 64:96], sin_t_ref[0, 64:96]
    for h in range(DSA_HEADS):
        r = DSA_ROW_Q + h * 128
        o1, o2 = _rope_cols(pt[r:r + 128], c64, s64)
        qt_ref[0, r:r + 64] = o1.astype(BF16)
        qt_ref[0, r + 64:r + 128] = o2.astype(BF16)
    for h in range(IDX_HEADS):
        r = DSA_ROW_QI + h * 64
        o1, o2 = _rope_cols(pt[r:r + 64], c32, s32)
        qit_ref[0, h * 64:h * 64 + 32] = (o1 * IDX_SCALE).astype(BF16)
        qit_ref[0, h * 64 + 32:h * 64 + 64] = (o2 * IDX_SCALE).astype(BF16)
    for g in range(DSA_KV_HEADS):
        r = DSA_ROW_K + g * 128
        o1, o2 = _rope_cols(pt[r:r + 128], c64, s64)
        k_ref[0, g] = jnp.concatenate([o1, o2], axis=0).T.astype(BF16)
    o1, o2 = _rope_cols(pt[DSA_ROW_KI:DSA_ROW_KI + 64], c32, s32)
    ki_pad = jnp.concatenate([o1, o2, pt[DSA_ROW_KI + 64:DSA_ROW_KI + 128]], axis=0)
    ki_ref[0] = ki_pad.T[:, :IDX_DIM].astype(BF16)
    vt_ref[0] = pt[DSA_ROW_V:DSA_ROW_V + 256].astype(BF16)
    wit_ref[0] = pt[DSA_ROW_WI:DSA_ROW_WI + 8] * (IDX_HEADS ** -0.5)


def _dsa_proj(x, cos_t, sin_t, g, w_in):
    B, S, _ = x.shape
    T = ROW_TILE
    c = 0
    q_w = w_in[:, c:c + 1024]; c += 1024
    k_w = w_in[:, c:c + 256]; c += 256
    v_w = w_in[:, c:c + 256]; c += 256
    qi_w = w_in[:, c:c + 512]; c += 512
    ki_w = w_in[:, c:c + 64]; c += 64
    wi_w = w_in[:, c:c + 8]; c += 8
    z_w = w_in[:, c:c + 1024]
    zeros = lambda n: jnp.zeros((D_MODEL, n), w_in.dtype)
    w_f = jnp.concatenate([q_w, qi_w, k_w, ki_w, zeros(64), v_w, wi_w, zeros(8)], axis=1).T.astype(BF16)
    assert w_f.shape == (DSA_ROWS, D_MODEL)
    full = lambda shape: pl.BlockSpec(shape, lambda b, t: (0,) * len(shape))
    return pl.pallas_call(
        _dsa_proj_kernel,
        grid=(B, S // T),
        in_specs=[
            pl.BlockSpec((1, T, D_MODEL), lambda b, t: (b, t, 0)),
            pl.BlockSpec((1, 96, T), lambda b, t: (b, 0, t)),
            pl.BlockSpec((1, 96, T), lambda b, t: (b, 0, t)),
            full((1, D_MODEL)), full(w_f.shape), full((D_MODEL, DSA_WIDTH)),
        ],
        out_specs=[
            pl.BlockSpec((1, DSA_WIDTH, T), lambda b, t: (b, 0, t)),
            pl.BlockSpec((1, IDX_HEADS * IDX_DIM, T), lambda b, t: (b, 0, t)),
            pl.BlockSpec((1, DSA_KV_HEADS, T, DSA_HEAD_DIM), lambda b, t: (b, 0, t, 0)),
            pl.BlockSpec((1, T, IDX_DIM), lambda b, t: (b, t, 0)),
            pl.BlockSpec((1, DSA_KV_HEADS * DSA_HEAD_DIM, T), lambda b, t: (b, 0, t)),
            pl.BlockSpec((1, IDX_HEADS, T), lambda b, t: (b, 0, t)),
            pl.BlockSpec((1, T, DSA_WIDTH), lambda b, t: (b, t, 0)),
        ],
        out_shape=[
            jax.ShapeDtypeStruct((B, DSA_WIDTH, S), BF16),
            jax.ShapeDtypeStruct((B, IDX_HEADS * IDX_DIM, S), BF16),
            jax.ShapeDtypeStruct((B, DSA_KV_HEADS, S, DSA_HEAD_DIM), BF16),
            jax.ShapeDtypeStruct((B, S, IDX_DIM), BF16),
            jax.ShapeDtypeStruct((B, DSA_KV_HEADS * DSA_HEAD_DIM, S), BF16),
            jax.ShapeDtypeStruct((B, IDX_HEADS, S), F32),
            jax.ShapeDtypeStruct((B, S, DSA_WIDTH), BF16),
        ],
        compiler_params=_params(),
        name="dsa_proj",
    )(x, cos_t, sin_t, g.reshape(1, -1), w_f, z_w.astype(BF16))


INT_MIN = np.int32(-2 ** 31)
WORD = 32
PLANE_ROWS = K_CHUNK // WORD


def _float_to_ukey(x):
    bits = lax.bitcast_convert_type(x, jnp.int32)
    return bits ^ ((bits >> 31) | INT_MIN)


def _ukey_to_float(u):
    bits = jnp.where(u < 0, u ^ INT_MIN, ~u)
    return lax.bitcast_convert_type(bits, F32)


def _bit_transpose32(words):
    a = list(words)
    j, m = 16, 0x0000FFFF
    while j:
        mask = np.uint32(m).astype(np.int32)
        k = 0
        while k < 32:
            t = (a[k] ^ lax.shift_right_logical(a[k + j], jnp.int32(j))) & mask
            a[k] = a[k] ^ t
            a[k + j] = a[k + j] ^ lax.shift_left(t, jnp.int32(j))
            k = (k + j + 1) & ~j
        j >>= 1
        m = (m ^ (m << j)) & 0xFFFFFFFF
    return a


def _dsa_attn_kernel(qt_ref, qit_ref, wit_ref, k_ref, vt_ref, ki_ref, o_ref,
                     sc_ref, planes_ref, t_ref, acc_ref, *, top_k):
    qb = pl.program_id(1)
    n_chunks = qb // (K_CHUNK // Q_TILE) + 1
    qpos = qb * Q_TILE + lax.broadcasted_iota(jnp.int32, (K_CHUNK, Q_TILE), 1)

    def chunk_start(c):
        return pl.multiple_of(c * K_CHUNK, K_CHUNK)

    qi_all = jnp.concatenate(
        [qit_ref[0, h * IDX_DIM:(h + 1) * IDX_DIM, :] for h in range(IDX_HEADS)], axis=1)
    w_rows = [wit_ref[0, h:h + 1, :] for h in range(IDX_HEADS)]

    def score_body(c, carry):
        start = chunk_start(c)
        ki = ki_ref[0, pl.ds(start, K_CHUNK), :]
        lg = jnp.dot(ki, qi_all, preferred_element_type=F32)
        sc = jnp.zeros((K_CHUNK, Q_TILE), F32)
        for h in range(IDX_HEADS):
            sc = sc + jnp.maximum(lg[:, h * Q_TILE:(h + 1) * Q_TILE], 0.0) * w_rows[h]
        kpos = start + lax.broadcasted_iota(jnp.int32, (K_CHUNK, Q_TILE), 0)
        sc = jnp.where(kpos <= qpos, sc, -jnp.inf)
        sc_ref[pl.ds(start, K_CHUNK), :] = sc
        u = _float_to_ukey(sc)
        for half in range(K_CHUNK // (8 * WORD)):
            base = half * 8 * WORD
            planes = _bit_transpose32([u[base + 8 * v:base + 8 * v + 8] for v in range(WORD)])
            row = pl.multiple_of(c * PLANE_ROWS + half * 8, 8)
            for i in range(WORD):
                planes_ref[i, pl.ds(row, 8), :] = planes[i]
        return carry

    lax.fori_loop(0, n_chunks, score_body, 0)

    n_rows = planes_ref.shape[1]

    def lane_count(words):
        pc = lax.population_count(words).astype(F32)
        a = jnp.sum(pc.reshape(4, n_rows // 4, Q_TILE), axis=0)
        a = jnp.sum(a.reshape(n_rows // 32, 8, Q_TILE), axis=0)
        return jnp.sum(a, axis=0, keepdims=True)

    def search(_):
        def zero_fill(c, carry):
            row = pl.multiple_of(c * PLANE_ROWS, PLANE_ROWS)
            planes_ref[:, pl.ds(row, PLANE_ROWS), :] = jnp.zeros((WORD, PLANE_ROWS, Q_TILE), jnp.int32)
            return carry
        lax.fori_loop(n_chunks, n_rows // PLANE_ROWS, zero_fill, 0)

        row_id = lax.broadcasted_iota(jnp.int32, (n_rows, Q_TILE), 0)
        alive0 = jnp.where(row_id < n_chunks * PLANE_ROWS, jnp.int32(-1), jnp.int32(0))

        def bit_body(i, carry):
            alive, rem, t = carry
            ones = alive & planes_ref[i]
            n = lane_count(ones)
            take = n >= rem
            alive = jnp.where(take, ones, alive ^ ones)
            rem = jnp.where(take, rem, rem - n)
            t = jnp.where(take, t | lax.shift_left(jnp.int32(1), 31 - i), t)
            return alive, rem, t

        alive, rem, t = lax.fori_loop(
            0, 32, bit_body,
            (alive0, jnp.full((1, Q_TILE), float(top_k), F32), jnp.zeros((1, Q_TILE), jnp.int32)))
        return _ukey_to_float(t), rem, lane_count(alive) - rem

    def take_all(_):
        return (jnp.full((1, Q_TILE), jnp.finfo(F32).min, F32), jnp.zeros((1, Q_TILE), F32),
                jnp.zeros((1, Q_TILE), F32))

    thr, need, surplus = lax.cond((qb + 1) * Q_TILE > top_k, search, take_all, 0)
    has_tie = jnp.max(surplus) > 0.0

    @pl.when(jnp.logical_not(has_tie))
    def _():
        def body(c, carry):
            sl = pl.ds(chunk_start(c), K_CHUNK)
            sc_ref[sl, :] = jnp.where(sc_ref[sl, :] >= thr, 0.0, NEG)
            return carry
        lax.fori_loop(0, n_chunks, body, 0)

    @pl.when(has_tie)
    def _():
        r_i = lax.broadcasted_iota(jnp.int32, (K_CHUNK, K_CHUNK), 0)
        c_i = lax.broadcasted_iota(jnp.int32, (K_CHUNK, K_CHUNK), 1)
        before = jnp.where(c_i < r_i, 1.0, 0.0).astype(BF16)

        def body(c, seen):
            sl = pl.ds(chunk_start(c), K_CHUNK)
            blk = sc_ref[sl, :]
            eq = jnp.where(blk == thr, 1.0, 0.0)
            rank = seen + jnp.dot(before, eq.astype(BF16), preferred_element_type=F32)
            keep = jnp.where(blk > thr, 1.0, jnp.where(rank < need, eq, 0.0))
            sc_ref[sl, :] = jnp.where(keep > 0.0, 0.0, NEG)
            return seen + jnp.sum(eq, axis=0, keepdims=True)
        lax.fori_loop(0, n_chunks, body, jnp.zeros((1, Q_TILE), F32))

    scale2 = DSA_HEAD_DIM ** -0.5 * LOG2E
    cols = DSA_GROUP * Q_TILE
    q_grp = [
        jnp.concatenate(
            [qt_ref[0, (g * DSA_GROUP + j) * DSA_HEAD_DIM:(g * DSA_GROUP + j + 1) * DSA_HEAD_DIM, :]
             for j in range(DSA_GROUP)], axis=1)
        for g in range(DSA_KV_HEADS)]

    def fold_rows(x, op):
        a = op(x.reshape(8, K_CHUNK // 8, x.shape[1]), axis=0)
        return op(a.reshape(8, 8, x.shape[1]), axis=0)

    def score_body(c, m8):
        start = chunk_start(c)
        bias = sc_ref[pl.ds(start, K_CHUNK), :]
        out = []
        for g in range(DSA_KV_HEADS):
            k = k_ref[0, g, pl.ds(start, K_CHUNK), :]
            s = jnp.dot(k, q_grp[g], preferred_element_type=F32)
            col_max = []
            for j in range(DSA_GROUP):
                t = s[:, j * Q_TILE:(j + 1) * Q_TILE] * scale2 + bias
                t_ref[g, pl.ds(start, K_CHUNK), j * Q_TILE:(j + 1) * Q_TILE] = t
                col_max.append(fold_rows(t, jnp.max))
            out.append(jnp.maximum(m8[g], jnp.concatenate(col_max, axis=1)))
        return tuple(out)

    m8 = lax.fori_loop(0, n_chunks, score_body,
                       tuple(jnp.full((8, cols), NEG, F32) for _ in range(DSA_KV_HEADS)))
    m_row = [jnp.max(m, axis=0, keepdims=True) for m in m8]
    acc_ref[...] = jnp.zeros(acc_ref.shape, F32)

    def pv_body(c, l8):
        start = chunk_start(c)
        out = []
        for g in range(DSA_KV_HEADS):
            p = jnp.exp2(t_ref[g, pl.ds(start, K_CHUNK), :] - m_row[g])
            vt = vt_ref[0, g * DSA_HEAD_DIM:(g + 1) * DSA_HEAD_DIM, pl.ds(start, K_CHUNK)]
            acc_ref[g] += jnp.dot(vt, p.astype(BF16), preferred_element_type=F32)
            out.append(l8[g] + fold_rows(p, jnp.sum))
        return tuple(out)

    l8 = lax.fori_loop(0, n_chunks, pv_body,
                       tuple(jnp.zeros((8, cols), F32) for _ in range(DSA_KV_HEADS)))
    for g in range(DSA_KV_HEADS):
        o_t = acc_ref[g] / jnp.sum(l8[g], axis=0, keepdims=True)
        for j in range(DSA_GROUP):
            h = g * DSA_GROUP + j
            o_ref[0, :, h * DSA_HEAD_DIM:(h + 1) * DSA_HEAD_DIM] = (
                o_t[:, j * Q_TILE:(j + 1) * Q_TILE].T.astype(BF16))


def _dsa_attn(qt, qit, wit, k, vt, ki):
    B, _, S = qt.shape
    top_k = min(TOPK_MAX, S // 4)
    cols = DSA_GROUP * Q_TILE
    return pl.pallas_call(
        functools.partial(_dsa_attn_kernel, top_k=top_k),
        grid=(B, S // Q_TILE),
        in_specs=[
            pl.BlockSpec((1, DSA_WIDTH, Q_TILE), lambda b, i: (b, 0, i)),
            pl.BlockSpec((1, IDX_HEADS * IDX_DIM, Q_TILE), lambda b, i: (b, 0, i)),
            pl.BlockSpec((1, IDX_HEADS, Q_TILE), lambda b, i: (b, 0, i)),
            pl.BlockSpec((1, DSA_KV_HEADS, S, DSA_HEAD_DIM), lambda b, i: (b, 0, 0, 0)),
            pl.BlockSpec((1, DSA_KV_HEADS * DSA_HEAD_DIM, S), lambda b, i: (b, 0, 0)),
            pl.BlockSpec((1, S, IDX_DIM), lambda b, i: (b, 0, 0)),
        ],
        out_specs=pl.BlockSpec((1, Q_TILE, DSA_WIDTH), lambda b, i: (b, i, 0)),
        out_shape=jax.ShapeDtypeStruct((B, S, DSA_WIDTH), BF16),
        scratch_shapes=[
            pltpu.VMEM((S, Q_TILE), F32),
            pltpu.VMEM((WORD, S // WORD, Q_TILE), jnp.int32),
            pltpu.VMEM((DSA_KV_HEADS, S, cols), F32),
            pltpu.VMEM((DSA_KV_HEADS, DSA_HEAD_DIM, cols), F32),
        ],
        compiler_params=_params(),
        name="dsa_attn",
    )(qt, qit, wit, k, vt, ki)


def _post_kernel(o_ref, z_ref, x_ref, p_ref, w_out_ref, pn_ref, w_gate_ref, w_proj_ref, fn_ref,
                 y_ref, *, final):
    z = z_ref[0].astype(F32)
    gated = o_ref[0].astype(F32) * (z / (1.0 + jnp.exp(-z)))
    x1 = x_ref[0] + jnp.dot(gated.astype(BF16), w_out_ref[...], preferred_element_type=F32)
    pre = jnp.dot(_rms(x1, pn_ref[...]).astype(BF16), w_gate_ref[...], preferred_element_type=F32)
    gate = 1.0 / (1.0 + jnp.exp(-pre))
    emb = jnp.dot(p_ref[0, 0].astype(BF16), w_proj_ref[...], preferred_element_type=F32)
    x2 = x1 + emb * gate
    y_ref[0] = _rms(x2, fn_ref[...]) if final else x2


def _post(o, z, x, p, layer, w_out, pn, w_gate, w_proj, fn, final):
    B, S, _ = x.shape
    T = ROW_TILE
    full = lambda shape: pl.BlockSpec(shape, lambda b, t: (0,) * len(shape))
    tok = lambda d: pl.BlockSpec((1, T, d), lambda b, t: (b, t, 0))
    return pl.pallas_call(
        functools.partial(_post_kernel, final=final),
        grid=(B, S // T),
        in_specs=[
            tok(D_MODEL), tok(D_MODEL), tok(D_MODEL),
            pl.BlockSpec((1, 1, T, PLE_DIM), lambda b, t: (layer, b, t, 0)),
            full((D_MODEL, D_MODEL)), full((1, D_MODEL)), full((D_MODEL, D_MODEL)),
            full((PLE_DIM, D_MODEL)), full((1, D_MODEL)),
        ],
        out_specs=tok(D_MODEL),
        out_shape=jax.ShapeDtypeStruct((B, S, D_MODEL), F32),
        compiler_params=_params(),
        name="post",
    )(o, z, x, p, w_out.astype(BF16), pn.reshape(1, -1), w_gate.astype(BF16), w_proj.astype(BF16),
      fn.reshape(1, -1))


def kernel(x, p, positions, norm_in, mla_w_in, mla_q_norm, mla_kv_norm, mla_w_uq, mla_w_uk, mla_w_uv,
           mla_w_out, dsa_w_in, dsa_w_out, ple_w_proj, ple_norm, ple_w_gate, final_norm):
    cos_t, sin_t, cos_r, sin_r = _rope_tables(positions)
    for i in range(DEPTH):
        j = i // 2
        if i % 2 == 0:
            q_cat, kv_cat, z = _mla_proj(x, cos_r, sin_r, norm_in[i], mla_w_in[j], mla_q_norm[j],
                                         mla_kv_norm[j], mla_w_uq[j], mla_w_uk[j])
            o = _mla_attn(q_cat, kv_cat, mla_w_uv[j])
            w_out = mla_w_out[j]
        else:
            qt, qit, k, ki, vt, wit, z = _dsa_proj(x, cos_t, sin_t, norm_in[i], dsa_w_in[j])
            o = _dsa_attn(qt, qit, wit, k, vt, ki)
            w_out = dsa_w_out[j]
        x = _post(o, z, x, p, i, w_out, ple_norm[i], ple_w_gate[i], ple_w_proj[i], final_norm,
                  final=(i == DEPTH - 1))
    return x
```
